```python
import jax
import jax.numpy as jnp
from jax import lax
import numpy as np


D_MODEL = 1024
BATCH = 8
SEQ = 4096
DEPTH = 1

GRID_W = 64
CTX_LEN = 256
D_MIX = D_MODEL
D_POOL = D_MIX // 2
POOL_WINDOWS = (2, 4, 8, 16)
POOL_GW = D_POOL // len(POOL_WINDOWS)
D_ATTN = D_MIX - D_POOL
NA_HEADS = 8
NA_HEAD_DIM = D_ATTN // NA_HEADS
NA_KH_MAX = 8
NA_KW = 16
D_IN = D_POOL + 3 * D_ATTN
PEER_HEADS = 8
PEER_NKEYS = 128
PEER_N_EXPERTS = PEER_NKEYS * PEER_NKEYS
PEER_TOPK = 16
PEER_DKEY = 256
PEER_TOKEN_BLOCK = 128
N_MOD = 6
EPS = 1e-6

kernel_name = 'hybrid_pool_natten_peer_dit_layer'


def rmsnorm(x, g):
    xf = x.astype(jnp.float32)
    y = xf * lax.rsqrt(jnp.mean(xf * xf, axis=-1, keepdims=True) + EPS)
    return (y * g.astype(jnp.float32)).astype(x.dtype)


def modulate(h, shift, scale):
    return h * (1 + scale) + shift


def multiscale_pool(p, w_grp, scale):
    B, L, _ = p.shape
    pf = p.astype(jnp.float32)
    cs = jnp.concatenate([jnp.zeros((B, 1, D_POOL), jnp.float32), jnp.cumsum(pf, axis=1)], axis=1)
    t = jnp.arange(L)
    outs = []
    for g, w in enumerate(POOL_WINDOWS):
        sl = slice(g * POOL_GW, (g + 1) * POOL_GW)
        lo = jnp.clip(t - w // 2, 0, L)
        hi = jnp.clip(t + w // 2, 0, L)
        csg = cs[..., sl]
        cnt = (hi - lo).astype(jnp.float32)[None, :, None]
        outs.append((jnp.take(csg, hi, axis=1) - jnp.take(csg, lo, axis=1)) / cnt - pf[..., sl])
    pooled = jnp.stack(outs, axis=2).astype(p.dtype)
    y = jnp.einsum('blgc,gcd->blgd', pooled, w_grp).reshape(B, L, D_POOL)
    return y * scale


def neighbourhood_attention(q, k, v, k_ctx, v_ctx, rpb):
    B, S, H, Dh = q.shape
    rows = S // GRID_W
    kh = min(NA_KH_MAX, rows)
    n_loc = kh * NA_KW
    qg = q.reshape(B, rows, GRID_W, H, Dh)
    kg = k.reshape(B, rows, GRID_W, H, Dh)
    vg = v.reshape(B, rows, GRID_W, H, Dh)
    col = jnp.arange(GRID_W)
    c0 = jnp.clip(col - NA_KW // 2, 0, GRID_W - NA_KW)
    col_idx = c0[:, None] + jnp.arange(NA_KW)[None, :]
    col_off = col_idx - col[:, None] + (NA_KW - 1)
    scale = Dh ** -0.5

    def row_block(r):
        r0 = jnp.clip(r - kh // 2, 0, rows - kh)
        k_rows = lax.dynamic_slice_in_dim(kg, r0, kh, axis=1)
        v_rows = lax.dynamic_slice_in_dim(vg, r0, kh, axis=1)
        k_win = k_rows[:, :, col_idx]
        v_win = v_rows[:, :, col_idx]
        q_r = lax.dynamic_index_in_dim(qg, r, axis=1, keepdims=False)
        row_off = r0 + jnp.arange(kh) - r + (NA_KH_MAX - 1)
        bias = rpb[:, row_off[None, :, None], col_off[:, None, :]].astype(jnp.float32)
        s_loc = jnp.einsum('bwhd,biwjhd->bhwij', q_r, k_win).astype(jnp.float32) * scale + bias
        s_ctx = jnp.einsum('bwhd,bchd->bhwc', q_r, k_ctx).astype(jnp.float32) * scale
        s_all = jnp.concatenate([s_loc.reshape(B, H, GRID_W, n_loc), s_ctx], axis=-1)
        prob = jax.nn.softmax(s_all, axis=-1)
        p_loc = prob[..., :n_loc].reshape(B, H, GRID_W, kh, NA_KW).astype(v.dtype)
        p_ctx = prob[..., n_loc:].astype(v.dtype)
        return (jnp.einsum('bhwij,biwjhd->bwhd', p_loc, v_win)
                + jnp.einsum('bhwc,bchd->bwhd', p_ctx, v_ctx))

    out = lax.map(row_block, jnp.arange(rows))
    return jnp.moveaxis(out, 0, 1).reshape(B, S, H * Dh)


def context_attention(q, k, v):
    B, C, H, Dh = q.shape
    s = jnp.einsum('bqhd,bkhd->bhqk', q, k).astype(jnp.float32) * (Dh ** -0.5)
    prob = jax.nn.softmax(s, axis=-1).astype(v.dtype)
    return jnp.einsum('bhqk,bkhd->bqhd', prob, v).reshape(B, C, H * Dh)


def peer(h, w_q, sub_keys, u, v):
    B, L, D = h.shape
    tok = h.reshape((B * L) // PEER_TOKEN_BLOCK, PEER_TOKEN_BLOCK, D)
    k = PEER_TOPK

    def block(xb):
        tb = xb.shape[0]
        q = (xb @ w_q).reshape(tb, PEER_HEADS, 2, PEER_DKEY // 2)
        s = jnp.einsum('thpd,hpnd->thpn', q, sub_keys).astype(jnp.float32)
        sv, si = lax.top_k(s, k)
        cand = (sv[:, :, 0, :, None] + sv[:, :, 1, None, :]).reshape(tb, PEER_HEADS, k * k)
        cand_idx = (si[:, :, 0, :, None] * PEER_NKEYS + si[:, :, 1, None, :]).reshape(tb, PEER_HEADS, k * k)
        top_v, top_i = lax.top_k(cand, k)
        experts = jnp.take_along_axis(cand_idx, top_i, axis=-1)
        g = jax.nn.softmax(top_v, axis=-1)
        u_e = u[experts]
        v_e = v[experts]
        a = jax.nn.gelu(jnp.einsum('td,thkd->thk', xb, u_e).astype(jnp.float32), approximate=False)
        return jnp.einsum('thk,thkd->td', (g * a).astype(xb.dtype), v_e)

    return lax.map(block, tok).reshape(B, L, D)


def split_heads(t):
    B, L, _ = t.shape
    return t.reshape(B, L, NA_HEADS, NA_HEAD_DIM)


def setup_inputs(seed: int = 0) -> dict:
    key = jax.random.key(seed)
    ks = jax.random.split(key, 18)
    f32 = jnp.float32
    D = D_MODEL

    def nrm(k, shape, s):
        return jax.random.normal(k, shape, f32) * s

    return {
        'x': nrm(ks[0], (BATCH, SEQ, D), 1.0),
        'c': nrm(ks[1], (BATCH, D), 1.0),
        'ctx': nrm(ks[2], (BATCH, CTX_LEN, D), 1.0),
        'c_ctx': nrm(ks[3], (D,), 1.0),
        'ada_w': nrm(ks[4], (DEPTH, D, N_MOD * D), 0.5 * D ** -0.5),
        'ada_b': nrm(ks[5], (DEPTH, N_MOD * D), 0.01),
        'norm1_g': 1.0 + nrm(ks[6], (DEPTH, D), 0.05),
        'w_in': nrm(ks[7], (DEPTH, D, D_IN), D ** -0.5),
        'pool_w': nrm(ks[8], (DEPTH, len(POOL_WINDOWS), POOL_GW, POOL_GW), POOL_GW ** -0.5),
        'pool_scale': 1.0 + nrm(ks[9], (DEPTH, D_POOL), 0.1),
        'na_rpb': nrm(ks[10], (DEPTH, NA_HEADS, 2 * NA_KH_MAX - 1, 2 * NA_KW - 1), 0.2),
        'w_out': nrm(ks[11], (DEPTH, D_MIX, D), D_MIX ** -0.5),
        'norm2_g': 1.0 + nrm(ks[12], (DEPTH, D), 0.05),
        'peer_wq': nrm(ks[13], (DEPTH, D, PEER_HEADS * PEER_DKEY), D ** -0.5),
        'peer_keys': nrm(ks[14], (DEPTH, PEER_HEADS, 2, PEER_NKEYS, PEER_DKEY // 2), (PEER_DKEY // 2) ** -0.5),
        'peer_u': nrm(ks[15], (DEPTH, PEER_N_EXPERTS, D), D ** -0.5),
        'peer_v': nrm(ks[16], (DEPTH, PEER_N_EXPERTS, D), 0.5),
        'final_g': 1.0 + nrm(ks[17], (D,), 0.05),
    }


def reference(x, c, ctx, c_ctx, ada_w, ada_b, norm1_g, w_in, pool_w, pool_scale, na_rpb, w_out,
              norm2_g, peer_wq, peer_keys, peer_u, peer_v, final_g):
    kv_cols = slice(D_POOL + D_ATTN, D_IN)
    for layer in range(DEPTH):
        last = layer == DEPTH - 1
        mod_x = (jax.nn.silu(c) @ ada_w[layer] + ada_b[layer])[:, None, :]
        mod_c = jax.nn.silu(c_ctx) @ ada_w[layer] + ada_b[layer]
        sh1, sc1, g1, sh2, sc2, g2 = jnp.split(mod_x, N_MOD, axis=-1)
        csh1, csc1, cg1, csh2, csc2, cg2 = jnp.split(mod_c, N_MOD, axis=-1)

        hx = modulate(rmsnorm(x, norm1_g[layer]), sh1, sc1)
        hc = modulate(rmsnorm(ctx, norm1_g[layer]), csh1, csc1)
        zx = hx @ w_in[layer]
        px = zx[..., :D_POOL]
        qx = split_heads(zx[..., D_POOL:D_POOL + D_ATTN])
        kx = split_heads(zx[..., D_POOL + D_ATTN:D_POOL + 2 * D_ATTN])
        vx = split_heads(zx[..., D_POOL + 2 * D_ATTN:])
        zc_kv = hc @ w_in[layer][:, kv_cols]
        kc = split_heads(zc_kv[..., :D_ATTN])
        vc = split_heads(zc_kv[..., D_ATTN:])

        pool_x = multiscale_pool(px, pool_w[layer], pool_scale[layer])
        attn_x = neighbourhood_attention(qx, kx, vx, kc, vc, na_rpb[layer])
        x = x + g1 * (jnp.concatenate([pool_x, attn_x], axis=-1) @ w_out[layer])

        if not last:
            zc_pq = hc @ w_in[layer][:, :D_POOL + D_ATTN]
            pool_c = multiscale_pool(zc_pq[..., :D_POOL], pool_w[layer], pool_scale[layer])
            attn_c = context_attention(split_heads(zc_pq[..., D_POOL:]), kc, vc)
            ctx = ctx + cg1 * (jnp.concatenate([pool_c, attn_c], axis=-1) @ w_out[layer])

        h2 = modulate(rmsnorm(x, norm2_g[layer]), sh2, sc2)
        x = x + g2 * peer(h2, peer_wq[layer], peer_keys[layer], peer_u[layer], peer_v[layer])
        if not last:
            h2c = modulate(rmsnorm(ctx, norm2_g[layer]), csh2, csc2)
            ctx = ctx + cg2 * peer(h2c, peer_wq[layer], peer_keys[layer], peer_u[layer], peer_v[layer])

    return rmsnorm(x, final_g)
```

```python
import functools

import jax
import jax.numpy as jnp
import numpy as np
from jax import lax
from jax.experimental import pallas as pl
from jax.experimental.pallas import tpu as pltpu

F32 = jnp.float32
BF16 = jnp.bfloat16
I32 = jnp.int32
U32 = jnp.uint32

EPS = 1e-6
GRID_W = 64
POOL_WINDOWS = (2, 4, 8, 16)
NA_HEADS = 8
NA_KH = 8
NA_KW = 16
N_MOD = 6
PEER_HEADS = 8
PEER_NKEYS = 128
PEER_TOPK = 16

LANES = 128
SUBLANES = 8
ROWS_PER_QBLOCK = 4
ROWS_PER_KWINDOW = 12
NEG = -1e30
SLAB = 4
TABLE_PAD = 8
VMEM_TABLE_LIMIT = 48 * 1024 * 1024

CAND_COUNTS = tuple(PEER_TOPK // (i + 1) for i in range(PEER_TOPK))
N_CAND = sum(CAND_COUNTS)
N_CAND_PAD = -(-N_CAND // SUBLANES) * SUBLANES


def _nt_dot(a, b):
    return lax.dot_general(a, b, (((1,), (1,)), ((), ())), preferred_element_type=F32)


def _ada_kernel(c_ref, w_ref, b_ref, o_ref):
    s = jax.nn.silu(c_ref[...])
    o_ref[...] = jnp.dot(s.astype(BF16), w_ref[...].astype(BF16), preferred_element_type=F32) + b_ref[...]


def _ada(cc, w, b):
    m, d = cc.shape
    n = w.shape[1]
    return pl.pallas_call(
        _ada_kernel,
        grid=(n // d,),
        in_specs=[
            pl.BlockSpec((m, d), lambda j: (0, 0)),
            pl.BlockSpec((d, d), lambda j: (0, j)),
            pl.BlockSpec((1, d), lambda j: (0, j)),
        ],
        out_specs=pl.BlockSpec((m, d), lambda j: (0, j)),
        out_shape=jax.ShapeDtypeStruct((m, n), F32),
        name="ada",
    )(cc, w, b.reshape(1, n))


def _norm_mod(x, g, shift, scale):
    ms = jnp.mean(x * x, axis=-1, keepdims=True)
    y = x * lax.rsqrt(ms + EPS) * g
    return y * (1.0 + scale) + shift


def _in_proj_kernel(x_ref, mod_ref, g_ref, w_ref, *out_refs, splits):
    m = mod_ref[0]
    h = _norm_mod(x_ref[0], g_ref[...], m[0:1], m[1:2])
    z = jnp.dot(h.astype(BF16), w_ref[...], preferred_element_type=F32)
    for o_ref, (a, b, mult) in zip(out_refs, splits):
        v = z[:, a:b]
        if mult != 1.0:
            v = v * mult
        o_ref[0] = v.astype(o_ref.dtype)


def _in_proj(x, mod, g, w, splits, dtypes, tm, mod_per_batch, name):
    bsz, s, d = x.shape
    n = w.shape[1]
    mod_map = (lambda b, i: (b, 0, 0)) if mod_per_batch else (lambda b, i: (0, 0, 0))
    return pl.pallas_call(
        functools.partial(_in_proj_kernel, splits=splits),
        grid=(bsz, s // tm),
        in_specs=[
            pl.BlockSpec((1, tm, d), lambda b, i: (b, i, 0)),
            pl.BlockSpec((1, N_MOD, d), mod_map),
            pl.BlockSpec((1, d), lambda b, i: (0, 0)),
            pl.BlockSpec((d, n), lambda b, i: (0, 0)),
        ],
        out_specs=[pl.BlockSpec((1, tm, b - a), lambda b_, i: (b_, i, 0)) for (a, b, _) in splits],
        out_shape=[jax.ShapeDtypeStruct((bsz, s, b - a), dt) for (a, b, _), dt in zip(splits, dtypes)],
        name=name,
    )(x, mod, g.reshape(1, d), w)


def _attn_bias(rpb, rows):
    qr = jnp.arange(ROWS_PER_QBLOCK)[:, None, None, None]
    qc = jnp.arange(GRID_W)[None, :, None, None]
    j = jnp.arange(ROWS_PER_KWINDOW)[None, None, :, None]
    kc = jnp.arange(GRID_W)[None, None, None, :]
    c0 = jnp.clip(qc - NA_KW // 2, 0, GRID_W - NA_KW)
    col_ok = (kc >= c0) & (kc < c0 + NA_KW)
    col_idx = jnp.clip(kc - qc + NA_KW - 1, 0, 2 * NA_KW - 2)
    out = []
    for off, j0 in ((0, jnp.zeros_like(qr)), (-4, qr), (-8, jnp.full_like(qr, 4))):
        row_ok = (j >= j0) & (j < j0 + NA_KH)
        row_idx = jnp.clip(off + j - qr + NA_KH - 1, 0, 2 * NA_KH - 2)
        shape = (ROWS_PER_QBLOCK, GRID_W, ROWS_PER_KWINDOW, GRID_W)
        ri = jnp.broadcast_to(row_idx, shape)
        ci = jnp.broadcast_to(col_idx, shape)
        ok = jnp.broadcast_to(row_ok & col_ok, shape)
        b = jnp.where(ok[None], rpb[:, ri, ci].astype(F32), NEG)
        out.append(b.reshape(rpb.shape[0], ROWS_PER_QBLOCK * GRID_W, ROWS_PER_KWINDOW * GRID_W))
    return jnp.stack(out)


def _attn_kernel(q_ref, k0_ref, k1_ref, k2_ref, v0_ref, v1_ref, v2_ref, kc_ref, vc_ref, bias_ref, o_ref):
    tq = q_ref.shape[1]
    low = lax.broadcasted_iota(I32, (tq, LANES), 1) < (LANES // 2)
    for hp in range(NA_HEADS // 2):
        sl = slice(LANES * hp, LANES * (hp + 1))
        q = q_ref[0, :, sl]
        kw = jnp.concatenate([k0_ref[0, :, sl], k1_ref[0, :, sl], k2_ref[0, :, sl]], axis=0)
        vw = jnp.concatenate([v0_ref[0, :, sl], v1_ref[0, :, sl], v2_ref[0, :, sl]], axis=0)
        kcp = kc_ref[0, :, sl]
        vcp = vc_ref[0, :, sl]
        outs = []
        for hh in range(2):
            qm = jnp.where(low if hh == 0 else jnp.logical_not(low), q, jnp.zeros_like(q))
            s_loc = _nt_dot(qm, kw) + bias_ref[0, 2 * hp + hh]
            s_ctx = _nt_dot(qm, kcp)
            m = jnp.maximum(jnp.max(s_loc, axis=-1, keepdims=True), jnp.max(s_ctx, axis=-1, keepdims=True))
            e_loc = jnp.exp(s_loc - m)
            e_ctx = jnp.exp(s_ctx - m)
            den = jnp.sum(e_loc, axis=-1, keepdims=True) + jnp.sum(e_ctx, axis=-1, keepdims=True)
            o = jnp.dot(e_loc.astype(BF16), vw, preferred_element_type=F32)
            o = o + jnp.dot(e_ctx.astype(BF16), vcp, preferred_element_type=F32)
            outs.append(o / den)
        o_ref[0, :, sl] = jnp.where(low, outs[0], outs[1]).astype(o_ref.dtype)


def _attention(q, k, v, kc, vc, bias):
    bsz, s, da = q.shape
    c = kc.shape[1]
    tq = ROWS_PER_QBLOCK * GRID_W
    nb = s // tq
    assert nb >= 3, "needs at least 12 grid rows"

    def kv_map(jj):
        return lambda i, b: (b, jnp.clip(i - 1, 0, nb - 3) + jj, 0)

    def bias_map(i, b):
        return (jnp.where(i == 0, 0, jnp.where(i == nb - 1, 2, 1)), 0, 0, 0)

    kv_specs = [pl.BlockSpec((1, tq, da), kv_map(jj)) for jj in range(3)]
    return pl.pallas_call(
        _attn_kernel,
        grid=(nb, bsz),
        in_specs=[pl.BlockSpec((1, tq, da), lambda i, b: (b, i, 0))] + kv_specs + kv_specs + [
            pl.BlockSpec((1, c, da), lambda i, b: (b, 0, 0)),
            pl.BlockSpec((1, c, da), lambda i, b: (b, 0, 0)),
            pl.BlockSpec((1, NA_HEADS, tq, ROWS_PER_KWINDOW * GRID_W), bias_map),
        ],
        out_specs=pl.BlockSpec((1, tq, da), lambda i, b: (b, i, 0)),
        out_shape=jax.ShapeDtypeStruct((bsz, s, da), BF16),
        compiler_params=pltpu.CompilerParams(vmem_limit_bytes=VMEM_TABLE_LIMIT),
        name="attn",
    )(q, k, k, k, v, v, v, kc, vc, bias)


def _mix_kernel(p_ref, pprev_ref, pnext_ref, a_ref, x_ref, mod_ref, pw_ref, ps_ref, wo_ref, g2_ref,
                x1_ref, h2_ref, *, seq_len):
    tm = p_ref.shape[1]
    dp = p_ref.shape[2]
    i = pl.program_id(1)
    last = pl.num_programs(1) - 1
    p = p_ref[0]
    prev = jnp.where(i > 0, pprev_ref[0], 0.0)
    nxt = jnp.where(i < last, pnext_ref[0], 0.0)
    pe = jnp.concatenate([prev, p, nxt], axis=0)
    t = i * tm + lax.broadcasted_iota(I32, (tm, 1), 0)
    ys = []
    for g, w in enumerate(POOL_WINDOWS):
        hw = w // 2
        sl = slice(LANES * g, LANES * (g + 1))
        pg = pe[:, sl]
        acc = pg[SUBLANES - hw:SUBLANES - hw + tm]
        for d in range(-hw + 1, hw):
            acc = acc + pg[SUBLANES + d:SUBLANES + d + tm]
        cnt = (jnp.minimum(t + hw, seq_len) - jnp.maximum(t - hw, 0)).astype(F32)
        pooled = acc / cnt - p[:, sl]
        y = jnp.dot(pooled.astype(BF16), pw_ref[g], preferred_element_type=F32)
        ys.append((y * ps_ref[:, sl]).astype(BF16))
    pool_x = jnp.concatenate(ys, axis=-1)
    mixed = jnp.dot(pool_x, wo_ref[0:dp, :], preferred_element_type=F32)
    mixed = mixed + jnp.dot(a_ref[0], wo_ref[dp:, :], preferred_element_type=F32)
    m = mod_ref[0]
    x1 = x_ref[0] + m[2:3] * mixed
    x1_ref[0] = x1
    h2_ref[0] = _norm_mod(x1, g2_ref[...], m[3:4], m[4:5]).astype(h2_ref.dtype)


def _mix(p, attn, x, mod, pool_w, pool_scale, w_out, norm2_g, tm):
    bsz, s, d = x.shape
    dp = p.shape[2]
    da = attn.shape[2]
    nblk8 = s // SUBLANES
    r = tm // SUBLANES
    return pl.pallas_call(
        functools.partial(_mix_kernel, seq_len=s),
        grid=(bsz, s // tm),
        in_specs=[
            pl.BlockSpec((1, tm, dp), lambda b, i: (b, i, 0)),
            pl.BlockSpec((1, SUBLANES, dp), lambda b, i: (b, jnp.maximum(i * r - 1, 0), 0)),
            pl.BlockSpec((1, SUBLANES, dp), lambda b, i: (b, jnp.minimum((i + 1) * r, nblk8 - 1), 0)),
            pl.BlockSpec((1, tm, da), lambda b, i: (b, i, 0)),
            pl.BlockSpec((1, tm, d), lambda b, i: (b, i, 0)),
            pl.BlockSpec((1, N_MOD, d), lambda b, i: (b, 0, 0)),
            pl.BlockSpec(pool_w.shape, lambda b, i: (0, 0, 0)),
            pl.BlockSpec((1, dp), lambda b, i: (0, 0)),
            pl.BlockSpec(w_out.shape, lambda b, i: (0, 0)),
            pl.BlockSpec((1, d), lambda b, i: (0, 0)),
        ],
        out_specs=[
            pl.BlockSpec((1, tm, d), lambda b, i: (b, i, 0)),
            pl.BlockSpec((1, tm, d), lambda b, i: (b, i, 0)),
        ],
        out_shape=[jax.ShapeDtypeStruct((bsz, s, d), F32), jax.ShapeDtypeStruct((bsz, s, d), BF16)],
        name="mix",
    )(p, p, p, attn, x, mod, pool_w, pool_scale.reshape(1, dp), w_out, norm2_g.reshape(1, d))


def _retrieve_kernel(h_ref, wqt_ref, keys_ref, ro_ref, g_ref, qt_scr, sv_scr, si_scr, cand_scr, cidx_scr):
    tq = h_ref.shape[0]
    n_half = tq // LANES
    n_hp = keys_ref.shape[0]
    qt_scr[...] = _nt_dot(wqt_ref[...], h_ref[...])
    kiota = lax.broadcasted_iota(I32, (PEER_NKEYS, LANES), 0)
    riota = lax.broadcasted_iota(I32, (N_CAND_PAD, LANES), 0)
    neg_inf = jnp.float32(-jnp.inf)

    def first_level(hp, carry):
        for half in range(n_half):
            lanes = slice(half * LANES, (half + 1) * LANES)
            q = qt_scr[pl.ds(pl.multiple_of(hp * PEER_NKEYS, PEER_NKEYS), PEER_NKEYS), lanes]
            s = jnp.dot(keys_ref[hp], q.astype(BF16), preferred_element_type=F32)

            vals, idxs = [], []
            for _ in range(PEER_TOPK):
                m = jnp.max(s, axis=0, keepdims=True)
                idx = jnp.min(jnp.where(s == m, kiota, PEER_NKEYS), axis=0, keepdims=True)
                vals.append(m)
                idxs.append(idx)
                s = jnp.where(kiota == idx, neg_inf, s)
            sv_scr[hp, :, lanes] = jnp.concatenate(vals, axis=0)
            si_scr[hp, :, lanes] = jnp.concatenate(idxs, axis=0)
        return carry

    lax.fori_loop(0, n_hp, first_level, 0)

    def second_level(h, carry):
        for half in range(n_half):
            lanes = slice(half * LANES, (half + 1) * LANES)
            sv0 = sv_scr[2 * h, :, lanes]
            sv1 = sv_scr[2 * h + 1, :, lanes]
            si0 = si_scr[2 * h, :, lanes]
            si1 = si_scr[2 * h + 1, :, lanes]
            off = 0
            for i, n in enumerate(CAND_COUNTS):
                cand_scr[off:off + n, :] = sv0[i:i + 1, :] + sv1[0:n, :]
                cidx_scr[off:off + n, :] = si0[i:i + 1, :] * PEER_NKEYS + si1[0:n, :]
                off += n
            if N_CAND_PAD > N_CAND:
                cand_scr[N_CAND:, :] = jnp.full((N_CAND_PAD - N_CAND, LANES), neg_inf, F32)
                cidx_scr[N_CAND:, :] = jnp.zeros((N_CAND_PAD - N_CAND, LANES), I32)
            cidx = cidx_scr[...]

            cand = cand_scr[...]
            vals, exps = [], []
            for _ in range(PEER_TOPK):
                m = jnp.max(cand, axis=0, keepdims=True)
                pos = jnp.min(jnp.where(cand == m, riota, N_CAND_PAD), axis=0, keepdims=True)
                sel = riota == pos
                vals.append(m)
                exps.append(jnp.max(jnp.where(sel, cidx, -1), axis=0, keepdims=True))
                cand = jnp.where(sel, neg_inf, cand)
            tv = jnp.concatenate(vals, axis=0)
            e = jnp.exp(tv - vals[0])
            rows = pl.ds(pl.multiple_of(h * PEER_TOPK, PEER_TOPK), PEER_TOPK)
            g_ref[half, rows, :] = e / jnp.sum(e, axis=0, keepdims=True)
            ro_ref[half, rows, :] = jnp.concatenate(exps, axis=0) * SLAB + (TABLE_PAD - SLAB)
        return carry

    lax.fori_loop(0, n_hp // 2, second_level, 0)


def _retrieve(h2, wqt, keys, tq):
    n, d = h2.shape
    nq = wqt.shape[0]
    n_hp = keys.shape[0]
    blk = (tq // LANES, PEER_HEADS * PEER_TOPK, LANES)
    out = jax.ShapeDtypeStruct((n // LANES, PEER_HEADS * PEER_TOPK, LANES), I32)
    return pl.pallas_call(
        _retrieve_kernel,
        grid=(n // tq,),
        in_specs=[
            pl.BlockSpec((tq, d), lambda i: (i, 0)),
            pl.BlockSpec((nq, d), lambda i: (0, 0)),
            pl.BlockSpec(keys.shape, lambda i: (0, 0, 0)),
        ],
        out_specs=[pl.BlockSpec(blk, lambda i: (i, 0, 0)), pl.BlockSpec(blk, lambda i: (i, 0, 0))],
        out_shape=[out, jax.ShapeDtypeStruct(out.shape, F32)],
        scratch_shapes=[
            pltpu.VMEM((nq, tq), F32),
            pltpu.VMEM((n_hp, PEER_TOPK, tq), F32),
            pltpu.VMEM((n_hp, PEER_TOPK, tq), I32),
            pltpu.VMEM((N_CAND_PAD, LANES), F32),
            pltpu.VMEM((N_CAND_PAD, LANES), I32),
        ],
        name="retrieve",
    )(h2, wqt, keys)


def _pack_kernel(t_ref, o_ref):
    o_ref[...] = pltpu.bitcast(t_ref[...].astype(BF16), U32)


def _pack_table(tab, te):
    e, d = tab.shape
    assert d == SUBLANES * LANES
    t8 = tab.reshape(e * SUBLANES, LANES)
    packed = pl.pallas_call(
        _pack_kernel,
        grid=(e // te,),
        in_specs=[pl.BlockSpec((te * SUBLANES, LANES), lambda i: (i, 0))],
        out_specs=pl.BlockSpec((te * SLAB, LANES), lambda i: (i, 0)),
        out_shape=jax.ShapeDtypeStruct((e * SLAB, LANES), U32),
        name="pack",
    )(t8)
    pad = jnp.zeros((TABLE_PAD, LANES), U32)
    return jnp.concatenate([pad, packed, pad], axis=0)


def _gather_slabs(ro_ref, tab_ref, t, n_picks):
    upper = lax.broadcasted_iota(I32, (SUBLANES, LANES), 0) >= SLAB
    pairs = []
    for k in range(0, n_picks, 2):
        lo = tab_ref[pl.ds(ro_ref[k, t] + SLAB, SUBLANES), :]
        hi = tab_ref[pl.ds(ro_ref[k + 1, t], SUBLANES), :]
        pairs.append(jnp.where(upper, hi, lo))
    return pltpu.bitcast(jnp.concatenate(pairs, axis=0), BF16)


def _peer_u_kernel(ro_ref, g_ref, h8_ref, tab_ref, sum8_ref, w_ref, a8_scr):
    n_picks, tt = g_ref.shape[1], g_ref.shape[2]
    eye = (lax.broadcasted_iota(I32, (SUBLANES, n_picks * SUBLANES), 1) % SUBLANES
           == lax.broadcasted_iota(I32, (SUBLANES, n_picks * SUBLANES), 0))

    def token_group(tg, carry):
        rows = []
        for tl in range(SUBLANES):
            t = tg * SUBLANES + tl
            bm = _gather_slabs(ro_ref, tab_ref, t, n_picks)
            x8 = h8_ref[pl.ds(pl.multiple_of(t * SUBLANES, SUBLANES), SUBLANES), :]
            r = _nt_dot(x8, bm)
            rows.append(jnp.sum(jnp.where(eye, r, 0.0), axis=0, keepdims=True))
        a8_scr[pl.ds(pl.multiple_of(tg * SUBLANES, SUBLANES), SUBLANES), :] = jnp.concatenate(rows, axis=0)
        return carry

    lax.fori_loop(0, tt // SUBLANES, token_group, 0)
    a8 = a8_scr[...]
    hi = a8.astype(BF16)
    lo = (a8 - hi.astype(F32)).astype(BF16)
    a = jnp.dot(hi, sum8_ref[...], preferred_element_type=F32) + jnp.dot(lo, sum8_ref[...], preferred_element_type=F32)
    gelu = 0.5 * a * (1.0 + lax.erf(a * np.float32(2.0 ** -0.5)))
    w_ref[...] = g_ref[0].T * gelu


def _peer_u(ro, g, h8, tab, sum8):
    nt, n_picks, tt = ro.shape
    return pl.pallas_call(
        _peer_u_kernel,
        grid=(nt,),
        in_specs=[
            pl.BlockSpec((None, n_picks, tt), lambda i: (i, 0, 0), memory_space=pltpu.SMEM),
            pl.BlockSpec((1, n_picks, tt), lambda i: (i, 0, 0)),
            pl.BlockSpec((tt * SUBLANES, LANES), lambda i: (i, 0)),
            pl.BlockSpec(tab.shape, lambda i: (0, 0), pipeline_mode=pl.Buffered(1)),
            pl.BlockSpec(sum8.shape, lambda i: (0, 0)),
        ],
        out_specs=pl.BlockSpec((tt, n_picks), lambda i: (i, 0)),
        out_shape=jax.ShapeDtypeStruct((nt * tt, n_picks), F32),
        scratch_shapes=[pltpu.VMEM((tt, n_picks * SUBLANES), F32)],
        compiler_params=pltpu.CompilerParams(vmem_limit_bytes=VMEM_TABLE_LIMIT),
        name="peer_u",
    )(ro, g, h8, tab, sum8)


def _peer_v_kernel(ro_ref, w_ref, tab_ref, rep8_ref, o8_ref, whi_scr, wlo_scr):
    tt, n_picks = w_ref.shape
    w = w_ref[...]
    hi = w.astype(BF16)
    lo = (w - hi.astype(F32)).astype(BF16)
    whi_scr[...] = jnp.dot(hi, rep8_ref[...], preferred_element_type=F32)
    wlo_scr[...] = jnp.dot(lo, rep8_ref[...], preferred_element_type=F32)
    eye = (lax.broadcasted_iota(I32, (SUBLANES, n_picks * SUBLANES), 1) % SUBLANES
           == lax.broadcasted_iota(I32, (SUBLANES, n_picks * SUBLANES), 0))

    def token(t, carry):
        bm = _gather_slabs(ro_ref, tab_ref, t, n_picks)
        w_hi = jnp.where(eye, whi_scr[pl.ds(t, 1), :], 0.0).astype(BF16)
        w_lo = jnp.where(eye, wlo_scr[pl.ds(t, 1), :], 0.0).astype(BF16)
        r = jnp.dot(jnp.concatenate([w_hi, w_lo], axis=0), bm, preferred_element_type=F32)
        o8_ref[pl.ds(pl.multiple_of(t * SUBLANES, SUBLANES), SUBLANES), :] = r[0:SUBLANES] + r[SUBLANES:]
        return carry

    lax.fori_loop(0, tt, token, 0)


def _peer_v(ro, w, tab, rep8):
    nt, n_picks, tt = ro.shape
    return pl.pallas_call(
        _peer_v_kernel,
        grid=(nt,),
        in_specs=[
            pl.BlockSpec((None, n_picks, tt), lambda i: (i, 0, 0), memory_space=pltpu.SMEM),
            pl.BlockSpec((tt, n_picks), lambda i: (i, 0)),
            pl.BlockSpec(tab.shape, lambda i: (0, 0), pipeline_mode=pl.Buffered(1)),
            pl.BlockSpec(rep8.shape, lambda i: (0, 0)),
        ],
        out_specs=pl.BlockSpec((tt * SUBLANES, LANES), lambda i: (i, 0)),
        out_shape=jax.ShapeDtypeStruct((nt * tt * SUBLANES, LANES), F32),
        scratch_shapes=[pltpu.VMEM((tt, n_picks * SUBLANES), F32), pltpu.VMEM((tt, n_picks * SUBLANES), F32)],
        compiler_params=pltpu.CompilerParams(vmem_limit_bytes=VMEM_TABLE_LIMIT),
        name="peer_v",
    )(ro, w, tab, rep8)


def _final_kernel(x_ref, o_ref, mod_ref, g_ref, y_ref):
    m = mod_ref[0]
    x = x_ref[0] + m[5:6] * o_ref[0]
    ms = jnp.mean(x * x, axis=-1, keepdims=True)
    y_ref[0] = x * lax.rsqrt(ms + EPS) * g_ref[...]


def _final(x1, o, mod, g, tm):
    bsz, s, d = x1.shape
    spec = pl.BlockSpec((1, tm, d), lambda b, i: (b, i, 0))
    return pl.pallas_call(
        _final_kernel,
        grid=(bsz, s // tm),
        in_specs=[spec, spec, pl.BlockSpec((1, N_MOD, d), lambda b, i: (b, 0, 0)),
                  pl.BlockSpec((1, d), lambda b, i: (0, 0))],
        out_specs=spec,
        out_shape=jax.ShapeDtypeStruct((bsz, s, d), F32),
        name="final",
    )(x1, o, mod, g.reshape(1, d))


def _layer(x, ctx, mods, norm1_g, w_in, pool_w, pool_scale, rpb, w_out, norm2_g, wq, keys, u, v):
    bsz, s, d = x.shape
    d_in = w_in.shape[1]
    d_pool = pool_scale.shape[0]
    d_attn = (d_in - d_pool) // 3
    mod_x = mods[:bsz].reshape(bsz, N_MOD, d)
    mod_c = mods[bsz:bsz + 1].reshape(1, N_MOD, d)
    w_in_b = w_in.astype(BF16)
    scale = float((d_attn // NA_HEADS) ** -0.5)

    tm = min(512, s)
    p, q, k, vv = _in_proj(
        x, mod_x, norm1_g, w_in_b,
        splits=((0, d_pool, 1.0), (d_pool, d_pool + d_attn, scale),
                (d_pool + d_attn, d_pool + 2 * d_attn, 1.0), (d_pool + 2 * d_attn, d_in, 1.0)),
        dtypes=(F32, BF16, BF16, BF16), tm=tm, mod_per_batch=True, name="in_proj")
    kc, vc = _in_proj(
        ctx, mod_c, norm1_g, w_in_b[:, d_pool + d_attn:],
        splits=((0, d_attn, 1.0), (d_attn, 2 * d_attn, 1.0)),
        dtypes=(BF16, BF16), tm=ctx.shape[1], mod_per_batch=False, name="ctx_proj")

    attn = _attention(q, k, vv, kc, vc, _attn_bias(rpb, s // GRID_W))
    x1, h2 = _mix(p, attn, x, mod_x, pool_w.astype(BF16), pool_scale, w_out.astype(BF16), norm2_g, tm)

    n = bsz * s
    h2f = h2.reshape(n, d)
    ro, g = _retrieve(h2f, wq.T.astype(BF16), keys.reshape(-1, PEER_NKEYS, keys.shape[-1]).astype(BF16),
                      tq=min(256, n))
    n_picks = PEER_HEADS * PEER_TOPK
    rep8 = jnp.asarray(np.repeat(np.eye(n_picks, dtype=np.float32), SUBLANES, axis=1), BF16)
    w = _peer_u(ro, g, h2f.reshape(n * SUBLANES, LANES), _pack_table(u, 512), rep8.T)
    o8 = _peer_v(ro, w, _pack_table(v, 512), rep8)
    return x1, o8.reshape(bsz, s, d), mod_x, tm


def kernel(x, c, ctx, c_ctx, ada_w, ada_b, norm1_g, w_in, pool_w, pool_scale, na_rpb, w_out, norm2_g, peer_wq,
           peer_keys, peer_u, peer_v, final_g):
    depth = ada_w.shape[0]
    assert depth == 1, "context-stream update of deeper stacks is not implemented"
    bsz, s, d = x.shape
    rows_c = -(-(bsz + 1) // SUBLANES) * SUBLANES
    cc = jnp.concatenate([c, c_ctx[None, :], jnp.zeros((rows_c - bsz - 1, d), F32)], axis=0)
    mods = _ada(cc, ada_w[0], ada_b[0])
    x1, o, mod_x, tm = _layer(x, ctx, mods, norm1_g[0], w_in[0], pool_w[0], pool_scale[0], na_rpb[0], w_out[0],
                              norm2_g[0], peer_wq[0], peer_keys[0], peer_u[0], peer_v[0])
    return _final(x1, o, mod_x, final_g, tm)
```

```python
import functools

import jax
import jax.numpy as jnp
import numpy as np
from jax import lax
from jax.experimental import pallas as pl
from jax.experimental.pallas import tpu as pltpu

F32 = jnp.float32
BF16 = jnp.bfloat16
I32 = jnp.int32
U32 = jnp.uint32

EPS = 1e-6
GRID_W = 64
POOL_WINDOWS = (2, 4, 8, 16)
NA_HEADS = 8
NA_KH = 8
NA_KW = 16
N_MOD = 6
PEER_HEADS = 8
PEER_NKEYS = 128
PEER_TOPK = 16

LANES = 128
SUBLANES = 8
ROWS_PER_QBLOCK = 4
ROWS_PER_KWINDOW = 12
NEG = -1e30
SLAB = 4
TABLE_PAD = 8
RO_SPLIT = 8
VMEM_TABLE_LIMIT = 48 * 1024 * 1024

CAND_COUNTS = tuple(PEER_TOPK // (i + 1) for i in range(PEER_TOPK))
N_CAND = sum(CAND_COUNTS)
N_CAND_PAD = -(-N_CAND // SUBLANES) * SUBLANES


def _nt_dot(a, b):
    return lax.dot_general(a, b, (((1,), (1,)), ((), ())), preferred_element_type=F32)


def _ada_kernel(c_ref, w_ref, b_ref, o_ref):
    s = jax.nn.silu(c_ref[...])
    o_ref[...] = jnp.dot(s.astype(BF16), w_ref[...].astype(BF16), preferred_element_type=F32) + b_ref[...]


def _ada(cc, w, b):
    m, d = cc.shape
    n = w.shape[1]
    return pl.pallas_call(
        _ada_kernel,
        grid=(n // d,),
        in_specs=[
            pl.BlockSpec((m, d), lambda j: (0, 0)),
            pl.BlockSpec((d, d), lambda j: (0, j)),
            pl.BlockSpec((1, d), lambda j: (0, j)),
        ],
        out_specs=pl.BlockSpec((m, d), lambda j: (0, j)),
        out_shape=jax.ShapeDtypeStruct((m, n), F32),
        name="ada",
    )(cc, w, b.reshape(1, n))


def _norm_mod(x, g, shift, scale):
    ms = jnp.mean(x * x, axis=-1, keepdims=True)
    y = x * lax.rsqrt(ms + EPS) * g
    return y * (1.0 + scale) + shift


def _in_proj_kernel(x_ref, mod_ref, g_ref, w_ref, *out_refs, splits):
    m = mod_ref[0]
    h = _norm_mod(x_ref[0], g_ref[...], m[0:1], m[1:2])
    z = jnp.dot(h.astype(BF16), w_ref[...], preferred_element_type=F32)
    for o_ref, (a, b, mult) in zip(out_refs, splits):
        v = z[:, a:b]
        if mult != 1.0:
            v = v * mult
        o_ref[0] = v.astype(o_ref.dtype)


def _in_proj(x, mod, g, w, splits, dtypes, tm, mod_per_batch, name):
    bsz, s, d = x.shape
    n = w.shape[1]
    mod_map = (lambda b, i: (b, 0, 0)) if mod_per_batch else (lambda b, i: (0, 0, 0))
    return pl.pallas_call(
        functools.partial(_in_proj_kernel, splits=splits),
        grid=(bsz, s // tm),
        in_specs=[
            pl.BlockSpec((1, tm, d), lambda b, i: (b, i, 0)),
            pl.BlockSpec((1, N_MOD, d), mod_map),
            pl.BlockSpec((1, d), lambda b, i: (0, 0)),
            pl.BlockSpec((d, n), lambda b, i: (0, 0)),
        ],
        out_specs=[pl.BlockSpec((1, tm, b - a), lambda b_, i: (b_, i, 0)) for (a, b, _) in splits],
        out_shape=[jax.ShapeDtypeStruct((bsz, s, b - a), dt) for (a, b, _), dt in zip(splits, dtypes)],
        name=name,
    )(x, mod, g.reshape(1, d), w)


def _attn_bias(rpb):
    n_heads = rpb.shape[0]
    m = np.arange(2 * NA_KW - 1)[:, None, None]
    qc = np.arange(GRID_W)[None, :, None]
    kc = np.arange(GRID_W)[None, None, :]
    onehot = (kc - qc + NA_KW - 1 == m).astype(np.float32)
    c0 = np.clip(qc - NA_KW // 2, 0, GRID_W - NA_KW)
    col_ok = (kc >= c0) & (kc < c0 + NA_KW)
    cb = jnp.einsum("ham,mqk->haqk", rpb.astype(F32), onehot, precision=lax.Precision.HIGHEST)
    cb = jnp.where(col_ok, cb, NEG)
    masked = jnp.full((n_heads, GRID_W, GRID_W), NEG, F32)
    out = []
    for off, j0 in ((0, lambda qr: 0), (-4, lambda qr: qr), (-8, lambda qr: 4)):
        per_row = []
        for qr in range(ROWS_PER_QBLOCK):
            blocks = [cb[:, off + j - qr + NA_KH - 1] if j0(qr) <= j < j0(qr) + NA_KH else masked
                      for j in range(ROWS_PER_KWINDOW)]
            per_row.append(jnp.concatenate(blocks, axis=-1))
        out.append(jnp.concatenate(per_row, axis=1))
    return jnp.stack(out)


def _attn_kernel(q_ref, k0_ref, k1_ref, k2_ref, v0_ref, v1_ref, v2_ref, kc_ref, vc_ref, bias_ref, o_ref):
    tq = q_ref.shape[1]
    low = lax.broadcasted_iota(I32, (tq, LANES), 1) < (LANES // 2)
    for hp in range(NA_HEADS // 2):
        sl = slice(LANES * hp, LANES * (hp + 1))
        q = q_ref[0, :, sl]
        kw = jnp.concatenate([k0_ref[0, :, sl], k1_ref[0, :, sl], k2_ref[0, :, sl]], axis=0)
        vw = jnp.concatenate([v0_ref[0, :, sl], v1_ref[0, :, sl], v2_ref[0, :, sl]], axis=0)
        kcp = kc_ref[0, :, sl]
        vcp = vc_ref[0, :, sl]
        outs = []
        for hh in range(2):
            qm = jnp.where(low if hh == 0 else jnp.logical_not(low), q, jnp.zeros_like(q))
            s_loc = _nt_dot(qm, kw) + bias_ref[0, 2 * hp + hh]
            s_ctx = _nt_dot(qm, kcp)
            m = jnp.maximum(jnp.max(s_loc, axis=-1, keepdims=True), jnp.max(s_ctx, axis=-1, keepdims=True))
            e_loc = jnp.exp(s_loc - m)
            e_ctx = jnp.exp(s_ctx - m)
            den = jnp.sum(e_loc, axis=-1, keepdims=True) + jnp.sum(e_ctx, axis=-1, keepdims=True)
            o = jnp.dot(e_loc.astype(BF16), vw, preferred_element_type=F32)
            o = o + jnp.dot(e_ctx.astype(BF16), vcp, preferred_element_type=F32)
            outs.append(o / den)
        o_ref[0, :, sl] = jnp.where(low, outs[0], outs[1]).astype(o_ref.dtype)


def _attention(q, k, v, kc, vc, bias):
    bsz, s, da = q.shape
    c = kc.shape[1]
    tq = ROWS_PER_QBLOCK * GRID_W
    nb = s // tq
    assert nb >= 3, "needs at least 12 grid rows"

    def kv_map(jj):
        return lambda i, b: (b, jnp.clip(i - 1, 0, nb - 3) + jj, 0)

    def bias_map(i, b):
        return (jnp.where(i == 0, 0, jnp.where(i == nb - 1, 2, 1)), 0, 0, 0)

    kv_specs = [pl.BlockSpec((1, tq, da), kv_map(jj)) for jj in range(3)]
    return pl.pallas_call(
        _attn_kernel,
        grid=(nb, bsz),
        in_specs=[pl.BlockSpec((1, tq, da), lambda i, b: (b, i, 0))] + kv_specs + kv_specs + [
            pl.BlockSpec((1, c, da), lambda i, b: (b, 0, 0)),
            pl.BlockSpec((1, c, da), lambda i, b: (b, 0, 0)),
            pl.BlockSpec((1, NA_HEADS, tq, ROWS_PER_KWINDOW * GRID_W), bias_map),
        ],
        out_specs=pl.BlockSpec((1, tq, da), lambda i, b: (b, i, 0)),
        out_shape=jax.ShapeDtypeStruct((bsz, s, da), BF16),
        compiler_params=pltpu.CompilerParams(vmem_limit_bytes=VMEM_TABLE_LIMIT),
        name="attn",
    )(q, k, k, k, v, v, v, kc, vc, bias)


def _mix_kernel(p_ref, pprev_ref, pnext_ref, a_ref, x_ref, mod_ref, pw_ref, ps_ref, wo_ref, g2_ref,
                x1_ref, h2_ref, *, seq_len):
    tm = p_ref.shape[1]
    dp = p_ref.shape[2]
    i = pl.program_id(1)
    last = pl.num_programs(1) - 1
    p = p_ref[0]
    prev = jnp.where(i > 0, pprev_ref[0], 0.0)
    nxt = jnp.where(i < last, pnext_ref[0], 0.0)
    pe = jnp.concatenate([prev, p, nxt], axis=0)
    t = i * tm + lax.broadcasted_iota(I32, (tm, 1), 0)
    ys = []
    for g, w in enumerate(POOL_WINDOWS):
        hw = w // 2
        sl = slice(LANES * g, LANES * (g + 1))
        pg = pe[:, sl]
        acc = pg[SUBLANES - hw:SUBLANES - hw + tm]
        for d in range(-hw + 1, hw):
            acc = acc + pg[SUBLANES + d:SUBLANES + d + tm]
        cnt = (jnp.minimum(t + hw, seq_len) - jnp.maximum(t - hw, 0)).astype(F32)
        pooled = acc / cnt - p[:, sl]
        y = jnp.dot(pooled.astype(BF16), pw_ref[g], preferred_element_type=F32)
        ys.append((y * ps_ref[:, sl]).astype(BF16))
    pool_x = jnp.concatenate(ys, axis=-1)
    mixed = jnp.dot(pool_x, wo_ref[0:dp, :], preferred_element_type=F32)
    mixed = mixed + jnp.dot(a_ref[0], wo_ref[dp:, :], preferred_element_type=F32)
    m = mod_ref[0]
    x1 = x_ref[0] + m[2:3] * mixed
    x1_ref[0] = x1
    h2_ref[0] = _norm_mod(x1, g2_ref[...], m[3:4], m[4:5]).astype(h2_ref.dtype)


def _mix(p, attn, x, mod, pool_w, pool_scale, w_out, norm2_g, tm):
    bsz, s, d = x.shape
    dp = p.shape[2]
    da = attn.shape[2]
    nblk8 = s // SUBLANES
    r = tm // SUBLANES
    return pl.pallas_call(
        functools.partial(_mix_kernel, seq_len=s),
        grid=(bsz, s // tm),
        in_specs=[
            pl.BlockSpec((1, tm, dp), lambda b, i: (b, i, 0)),
            pl.BlockSpec((1, SUBLANES, dp), lambda b, i: (b, jnp.maximum(i * r - 1, 0), 0)),
            pl.BlockSpec((1, SUBLANES, dp), lambda b, i: (b, jnp.minimum((i + 1) * r, nblk8 - 1), 0)),
            pl.BlockSpec((1, tm, da), lambda b, i: (b, i, 0)),
            pl.BlockSpec((1, tm, d), lambda b, i: (b, i, 0)),
            pl.BlockSpec((1, N_MOD, d), lambda b, i: (b, 0, 0)),
            pl.BlockSpec(pool_w.shape, lambda b, i: (0, 0, 0)),
            pl.BlockSpec((1, dp), lambda b, i: (0, 0)),
            pl.BlockSpec(w_out.shape, lambda b, i: (0, 0)),
            pl.BlockSpec((1, d), lambda b, i: (0, 0)),
        ],
        out_specs=[
            pl.BlockSpec((1, tm, d), lambda b, i: (b, i, 0)),
            pl.BlockSpec((1, tm, d), lambda b, i: (b, i, 0)),
        ],
        out_shape=[jax.ShapeDtypeStruct((bsz, s, d), F32), jax.ShapeDtypeStruct((bsz, s, d), BF16)],
        name="mix",
    )(p, p, p, attn, x, mod, pool_w, pool_scale.reshape(1, dp), w_out, norm2_g.reshape(1, d))


def _retrieve_kernel(h_ref, wqt_ref, keys_ref, ro_ref, g_ref, qt_scr, sv_scr, si_scr, cand_scr, cidx_scr,
                     gt_scr, et_scr):
    tq = h_ref.shape[0]
    n_half = tq // LANES
    n_hp = keys_ref.shape[0]
    n_picks = ro_ref.shape[1]
    qt_scr[...] = _nt_dot(wqt_ref[...], h_ref[...])
    kiota = lax.broadcasted_iota(I32, (PEER_NKEYS, LANES), 0)
    riota = lax.broadcasted_iota(I32, (N_CAND_PAD, LANES), 0)
    neg_inf = jnp.float32(-jnp.inf)

    def first_level(hp, carry):
        for half in range(n_half):
            lanes = slice(half * LANES, (half + 1) * LANES)
            q = qt_scr[pl.ds(pl.multiple_of(hp * PEER_NKEYS, PEER_NKEYS), PEER_NKEYS), lanes]
            s = jnp.dot(keys_ref[hp], q.astype(BF16), preferred_element_type=F32)

            vals, idxs = [], []
            for _ in range(PEER_TOPK):
                m = jnp.max(s, axis=0, keepdims=True)
                idx = jnp.min(jnp.where(s == m, kiota, PEER_NKEYS), axis=0, keepdims=True)
                vals.append(m)
                idxs.append(idx)
                s = jnp.where(kiota == idx, neg_inf, s)
            sv_scr[hp, :, lanes] = jnp.concatenate(vals, axis=0)
            si_scr[hp, :, lanes] = jnp.concatenate(idxs, axis=0)
        return carry

    lax.fori_loop(0, n_hp, first_level, 0)

    def second_level(h, carry):
        for half in range(n_half):
            lanes = slice(half * LANES, (half + 1) * LANES)
            sv0 = sv_scr[2 * h, :, lanes]
            sv1 = sv_scr[2 * h + 1, :, lanes]
            si0 = si_scr[2 * h, :, lanes]
            si1 = si_scr[2 * h + 1, :, lanes]
            off = 0
            for i, n in enumerate(CAND_COUNTS):
                cand_scr[off:off + n, :] = sv0[i:i + 1, :] + sv1[0:n, :]
                cidx_scr[off:off + n, :] = si0[i:i + 1, :] * PEER_NKEYS + si1[0:n, :]
                off += n
            if N_CAND_PAD > N_CAND:
                cand_scr[N_CAND:, :] = jnp.full((N_CAND_PAD - N_CAND, LANES), neg_inf, F32)
                cidx_scr[N_CAND:, :] = jnp.zeros((N_CAND_PAD - N_CAND, LANES), I32)
            cidx = cidx_scr[...]

            cand = cand_scr[...]
            vals, exps = [], []
            for _ in range(PEER_TOPK):
                m = jnp.max(cand, axis=0, keepdims=True)
                pos = jnp.min(jnp.where(cand == m, riota, N_CAND_PAD), axis=0, keepdims=True)
                sel = riota == pos
                vals.append(m)
                exps.append(jnp.max(jnp.where(sel, cidx, -1), axis=0, keepdims=True))
                cand = jnp.where(sel, neg_inf, cand)
            tv = jnp.concatenate(vals, axis=0)
            e = jnp.exp(tv - vals[0])
            rows = pl.ds(pl.multiple_of(h * PEER_TOPK, PEER_TOPK), PEER_TOPK)
            gt_scr[half, rows, :] = e / jnp.sum(e, axis=0, keepdims=True)
            et_scr[half, rows, :] = jnp.concatenate(exps, axis=0)
        return carry

    lax.fori_loop(0, n_hp // 2, second_level, 0)

    even_pick = lax.broadcasted_iota(I32, (LANES, n_picks), 1) % 2 == 0
    for half in range(n_half):
        toks = slice(half * LANES, (half + 1) * LANES)
        g_ref[toks, :] = gt_scr[half].T
        ro_ref[toks, :] = et_scr[half].T * SLAB + jnp.where(even_pick, TABLE_PAD, TABLE_PAD - SLAB)


def _retrieve(h2, wqt, keys, tq):
    n, d = h2.shape
    nq = wqt.shape[0]
    n_hp = keys.shape[0]
    n_picks = PEER_HEADS * PEER_TOPK
    return pl.pallas_call(
        _retrieve_kernel,
        grid=(n // tq,),
        in_specs=[
            pl.BlockSpec((tq, d), lambda i: (i, 0)),
            pl.BlockSpec((nq, d), lambda i: (0, 0)),
            pl.BlockSpec(keys.shape, lambda i: (0, 0, 0)),
        ],
        out_specs=[pl.BlockSpec((tq, n_picks), lambda i: (i, 0)), pl.BlockSpec((tq, n_picks), lambda i: (i, 0))],
        out_shape=[jax.ShapeDtypeStruct((n, n_picks), I32), jax.ShapeDtypeStruct((n, n_picks), F32)],
        scratch_shapes=[
            pltpu.VMEM((nq, tq), F32),
            pltpu.VMEM((n_hp, PEER_TOPK, tq), F32),
            pltpu.VMEM((n_hp, PEER_TOPK, tq), I32),
            pltpu.VMEM((N_CAND_PAD, LANES), F32),
            pltpu.VMEM((N_CAND_PAD, LANES), I32),
            pltpu.VMEM((tq // LANES, n_picks, LANES), F32),
            pltpu.VMEM((tq // LANES, n_picks, LANES), I32),
        ],
        name="retrieve",
    )(h2, wqt, keys)


def _pack_kernel(t_ref, o_ref):
    o_ref[...] = pltpu.bitcast(t_ref[...].astype(BF16), U32)


def _pack_table(tab, te):
    e, d = tab.shape
    assert d == SUBLANES * LANES
    t8 = tab.reshape(e * SUBLANES, LANES)
    packed = pl.pallas_call(
        _pack_kernel,
        grid=(e // te,),
        in_specs=[pl.BlockSpec((te * SUBLANES, LANES), lambda i: (i, 0))],
        out_specs=pl.BlockSpec((te * SLAB, LANES), lambda i: (i, 0)),
        out_shape=jax.ShapeDtypeStruct((e * SLAB, LANES), U32),
        name="pack",
    )(t8)
    pad = jnp.zeros((TABLE_PAD, LANES), U32)
    return jnp.concatenate([pad, packed, pad], axis=0)


def _split_offsets(ro):
    return [ro[:, j::RO_SPLIT].reshape(-1) for j in range(RO_SPLIT)]


def _gather_slabs(ro_refs, tab_ref, t, n_picks):
    upper = lax.broadcasted_iota(I32, (SUBLANES, LANES), 0) >= SLAB
    per_tok = n_picks // RO_SPLIT

    def row_offset(k):
        return ro_refs[k % RO_SPLIT][t * per_tok + k // RO_SPLIT]

    pairs = []
    for k in range(0, n_picks, 2):
        lo = tab_ref[pl.ds(row_offset(k), SUBLANES), :]
        hi = tab_ref[pl.ds(row_offset(k + 1), SUBLANES), :]
        pairs.append(jnp.where(upper, hi, lo))
    return pltpu.bitcast(jnp.concatenate(pairs, axis=0), BF16)


def _peer_u_kernel(*refs):
    ro_refs = refs[:RO_SPLIT]
    g_ref, h8_ref, tab_ref, sum8_ref, w_ref, a8_scr = refs[RO_SPLIT:]
    tt, n_picks = g_ref.shape
    eye = (lax.broadcasted_iota(I32, (SUBLANES, n_picks * SUBLANES), 1) % SUBLANES
           == lax.broadcasted_iota(I32, (SUBLANES, n_picks * SUBLANES), 0))

    def token_group(tg, carry):
        rows = []
        for tl in range(SUBLANES):
            t = tg * SUBLANES + tl
            bm = _gather_slabs(ro_refs, tab_ref, t, n_picks)
            x8 = h8_ref[pl.ds(pl.multiple_of(t * SUBLANES, SUBLANES), SUBLANES), :]
            r = _nt_dot(x8, bm)
            rows.append(jnp.sum(jnp.where(eye, r, 0.0), axis=0, keepdims=True))
        a8_scr[pl.ds(pl.multiple_of(tg * SUBLANES, SUBLANES), SUBLANES), :] = jnp.concatenate(rows, axis=0)
        return carry

    lax.fori_loop(0, tt // SUBLANES, token_group, 0)
    a8 = a8_scr[...]
    hi = a8.astype(BF16)
    lo = (a8 - hi.astype(F32)).astype(BF16)
    a = jnp.dot(hi, sum8_ref[...], preferred_element_type=F32) + jnp.dot(lo, sum8_ref[...], preferred_element_type=F32)
    gelu = 0.5 * a * (1.0 + lax.erf(a * np.float32(2.0 ** -0.5)))
    w_ref[...] = g_ref[...] * gelu


def _offset_specs(tt, n_picks):
    per_tile = tt * n_picks // RO_SPLIT
    return [pl.BlockSpec((per_tile,), lambda i: (i,), memory_space=pltpu.SMEM) for _ in range(RO_SPLIT)]


def _peer_u(ro_parts, g, h8, tab, sum8, tt):
    n, n_picks = g.shape
    nt = n // tt
    return pl.pallas_call(
        _peer_u_kernel,
        grid=(nt,),
        in_specs=_offset_specs(tt, n_picks) + [
            pl.BlockSpec((tt, n_picks), lambda i: (i, 0)),
            pl.BlockSpec((tt * SUBLANES, LANES), lambda i: (i, 0)),
            pl.BlockSpec(tab.shape, lambda i: (0, 0), pipeline_mode=pl.Buffered(1)),
            pl.BlockSpec(sum8.shape, lambda i: (0, 0)),
        ],
        out_specs=pl.BlockSpec((tt, n_picks), lambda i: (i, 0)),
        out_shape=jax.ShapeDtypeStruct((nt * tt, n_picks), F32),
        scratch_shapes=[pltpu.VMEM((tt, n_picks * SUBLANES), F32)],
        compiler_params=pltpu.CompilerParams(vmem_limit_bytes=VMEM_TABLE_LIMIT),
        name="peer_u",
    )(*ro_parts, g, h8, tab, sum8)


def _peer_v_kernel(*refs):
    ro_refs = refs[:RO_SPLIT]
    w_ref, tab_ref, rep8_ref, o8_ref, whi_scr, wlo_scr = refs[RO_SPLIT:]
    tt, n_picks = w_ref.shape
    w = w_ref[...]
    hi = w.astype(BF16)
    lo = (w - hi.astype(F32)).astype(BF16)
    whi_scr[...] = jnp.dot(hi, rep8_ref[...], preferred_element_type=F32)
    wlo_scr[...] = jnp.dot(lo, rep8_ref[...], preferred_element_type=F32)
    eye = (lax.broadcasted_iota(I32, (SUBLANES, n_picks * SUBLANES), 1) % SUBLANES
           == lax.broadcasted_iota(I32, (SUBLANES, n_picks * SUBLANES), 0))

    def token_group(tg, carry):
        for tl in range(SUBLANES):
            t = tg * SUBLANES + tl
            bm = _gather_slabs(ro_refs, tab_ref, t, n_picks)
            w_hi = jnp.where(eye, whi_scr[pl.ds(t, 1), :], 0.0).astype(BF16)
            w_lo = jnp.where(eye, wlo_scr[pl.ds(t, 1), :], 0.0).astype(BF16)
            r = jnp.dot(jnp.concatenate([w_hi, w_lo], axis=0), bm, preferred_element_type=F32)
            o8_ref[pl.ds(pl.multiple_of(t * SUBLANES, SUBLANES), SUBLANES), :] = r[0:SUBLANES] + r[SUBLANES:]
        return carry

    lax.fori_loop(0, tt // SUBLANES, token_group, 0)


def _peer_v(ro_parts, w, tab, rep8, tt):
    n, n_picks = w.shape
    nt = n // tt
    return pl.pallas_call(
        _peer_v_kernel,
        grid=(nt,),
        in_specs=_offset_specs(tt, n_picks) + [
            pl.BlockSpec((tt, n_picks), lambda i: (i, 0)),
            pl.BlockSpec(tab.shape, lambda i: (0, 0), pipeline_mode=pl.Buffered(1)),
            pl.BlockSpec(rep8.shape, lambda i: (0, 0)),
        ],
        out_specs=pl.BlockSpec((tt * SUBLANES, LANES), lambda i: (i, 0)),
        out_shape=jax.ShapeDtypeStruct((nt * tt * SUBLANES, LANES), F32),
        scratch_shapes=[pltpu.VMEM((tt, n_picks * SUBLANES), F32), pltpu.VMEM((tt, n_picks * SUBLANES), F32)],
        compiler_params=pltpu.CompilerParams(vmem_limit_bytes=VMEM_TABLE_LIMIT),
        name="peer_v",
    )(*ro_parts, w, tab, rep8)


def _final_kernel(x_ref, o_ref, mod_ref, g_ref, y_ref):
    m = mod_ref[0]
    x = x_ref[0] + m[5:6] * o_ref[0]
    ms = jnp.mean(x * x, axis=-1, keepdims=True)
    y_ref[0] = x * lax.rsqrt(ms + EPS) * g_ref[...]


def _final(x1, o, mod, g, tm):
    bsz, s, d = x1.shape
    spec = pl.BlockSpec((1, tm, d), lambda b, i: (b, i, 0))
    return pl.pallas_call(
        _final_kernel,
        grid=(bsz, s // tm),
        in_specs=[spec, spec, pl.BlockSpec((1, N_MOD, d), lambda b, i: (b, 0, 0)),
                  pl.BlockSpec((1, d), lambda b, i: (0, 0))],
        out_specs=spec,
        out_shape=jax.ShapeDtypeStruct((bsz, s, d), F32),
        name="final",
    )(x1, o, mod, g.reshape(1, d))


def _layer(x, ctx, mods, norm1_g, w_in, pool_w, pool_scale, rpb, w_out, norm2_g, wq, keys, u, v):
    bsz, s, d = x.shape
    d_in = w_in.shape[1]
    d_pool = pool_scale.shape[0]
    d_attn = (d_in - d_pool) // 3
    mod_x = mods[:bsz].reshape(bsz, N_MOD, d)
    mod_c = mods[bsz:bsz + 1].reshape(1, N_MOD, d)
    w_in_b = w_in.astype(BF16)
    scale = float((d_attn // NA_HEADS) ** -0.5)

    tm = min(512, s)
    p, q, k, vv = _in_proj(
        x, mod_x, norm1_g, w_in_b,
        splits=((0, d_pool, 1.0), (d_pool, d_pool + d_attn, scale),
                (d_pool + d_attn, d_pool + 2 * d_attn, 1.0), (d_pool + 2 * d_attn, d_in, 1.0)),
        dtypes=(F32, BF16, BF16, BF16), tm=tm, mod_per_batch=True, name="in_proj")
    kc, vc = _in_proj(
        ctx, mod_c, norm1_g, w_in_b[:, d_pool + d_attn:],
        splits=((0, d_attn, 1.0), (d_attn, 2 * d_attn, 1.0)),
        dtypes=(BF16, BF16), tm=ctx.shape[1], mod_per_batch=False, name="ctx_proj")

    attn = _attention(q, k, vv, kc, vc, _attn_bias(rpb))
    x1, h2 = _mix(p, attn, x, mod_x, pool_w.astype(BF16), pool_scale, w_out.astype(BF16), norm2_g, tm)

    n = bsz * s
    h2f = h2.reshape(n, d)
    ro, g = _retrieve(h2f, wq.T.astype(BF16), keys.reshape(-1, PEER_NKEYS, keys.shape[-1]).astype(BF16),
                      tq=min(256, n))
    n_picks = PEER_HEADS * PEER_TOPK
    rep8 = jnp.asarray(np.repeat(np.eye(n_picks, dtype=np.float32), SUBLANES, axis=1), BF16)
    tt = min(LANES, n)
    ro_parts = _split_offsets(ro)
    w = _peer_u(ro_parts, g, h2f.reshape(n * SUBLANES, LANES), _pack_table(u, 512), rep8.T, tt)
    o8 = _peer_v(ro_parts, w, _pack_table(v, 512), rep8, tt)
    return x1, o8.reshape(bsz, s, d), mod_x, tm


def kernel(x, c, ctx, c_ctx, ada_w, ada_b, norm1_g, w_in, pool_w, pool_scale, na_rpb, w_out, norm2_g, peer_wq,
           peer_keys, peer_u, peer_v, final_g):
    depth = ada_w.shape[0]
    assert depth == 1, "context-stream update of deeper stacks is not implemented"
    bsz, s, d = x.shape
    rows_c = -(-(bsz + 1) // SUBLANES) * SUBLANES
    cc = jnp.concatenate([c, c_ctx[None, :], jnp.zeros((rows_c - bsz - 1, d), F32)], axis=0)
    mods = _ada(cc, ada_w[0], ada_b[0])
    x1, o, mod_x, tm = _layer(x, ctx, mods, norm1_g[0], w_in[0], pool_w[0], pool_scale[0], na_rpb[0], w_out[0],
                              norm2_g[0], peer_wq[0], peer_keys[0], peer_u[0], peer_v[0])
    return _final(x1, o, mod_x, final_g, tm)
```

```python
import functools

import jax
import jax.numpy as jnp
import numpy as np
from jax import lax
from jax.experimental import pallas as pl
from jax.experimental.pallas import tpu as pltpu

F32 = jnp.float32
BF16 = jnp.bfloat16
I32 = jnp.int32
U32 = jnp.uint32

EPS = 1e-6
GRID_W = 64
POOL_WINDOWS = (2, 4, 8, 16)
NA_HEADS = 8
NA_KH = 8
NA_KW = 16
N_MOD = 6
PEER_HEADS = 8
PEER_NKEYS = 128
PEER_TOPK = 16

LANES = 128
SUBLANES = 8
ROWS_PER_QBLOCK = 4
ROWS_PER_KWINDOW = 12
NEG = -1e30
SLAB = 4
TABLE_PAD = 8
RO_SPLIT = 8
TABLE_FILL_ROWS = 256
VMEM_TABLE_LIMIT = 48 * 1024 * 1024

CAND_COUNTS = tuple(PEER_TOPK // (i + 1) for i in range(PEER_TOPK))
N_CAND = sum(CAND_COUNTS)


def _nt_dot(a, b):
    return lax.dot_general(a, b, (((1,), (1,)), ((), ())), preferred_element_type=F32)


def _ada_kernel(c_ref, w_ref, b_ref, o_ref):
    s = jax.nn.silu(c_ref[...])
    o_ref[...] = jnp.dot(s.astype(BF16), w_ref[...].astype(BF16), preferred_element_type=F32) + b_ref[...]


def _ada(cc, w, b):
    m, d = cc.shape
    n = w.shape[1]
    return pl.pallas_call(
        _ada_kernel,
        grid=(n // d,),
        in_specs=[
            pl.BlockSpec((m, d), lambda j: (0, 0)),
            pl.BlockSpec((d, d), lambda j: (0, j)),
            pl.BlockSpec((1, d), lambda j: (0, j)),
        ],
        out_specs=pl.BlockSpec((m, d), lambda j: (0, j)),
        out_shape=jax.ShapeDtypeStruct((m, n), F32),
        name="ada",
    )(cc, w, b.reshape(1, n))


def _rows_to_slabs(x, slab_ref):
    t = x.shape[0]
    for j in range(SUBLANES):
        slab_ref[pl.ds(j, t, stride=SUBLANES), :] = x[:, LANES * j:LANES * (j + 1)]


def _slabs_to_rows(slab_ref):
    t = slab_ref.shape[0] // SUBLANES
    return jnp.concatenate([slab_ref[pl.ds(j, t, stride=SUBLANES), :] for j in range(SUBLANES)], axis=-1)


def _norm_mod(x, g, shift, scale):
    ms = jnp.mean(x * x, axis=-1, keepdims=True)
    y = x * lax.rsqrt(ms + EPS) * g
    return y * (1.0 + scale) + shift


def _in_proj_kernel(x_ref, mod_ref, g_ref, w_ref, *out_refs, splits):
    m = mod_ref[0]
    h = _norm_mod(x_ref[0], g_ref[...], m[0:1], m[1:2])
    z = jnp.dot(h.astype(BF16), w_ref[...], preferred_element_type=F32)
    for o_ref, (a, b, mult) in zip(out_refs, splits):
        v = z[:, a:b]
        if mult != 1.0:
            v = v * mult
        o_ref[0] = v.astype(o_ref.dtype)


def _in_proj(x, mod, g, w, splits, dtypes, tm, mod_per_batch, name):
    bsz, s, d = x.shape
    n = w.shape[1]
    mod_map = (lambda b, i: (b, 0, 0)) if mod_per_batch else (lambda b, i: (0, 0, 0))
    return pl.pallas_call(
        functools.partial(_in_proj_kernel, splits=splits),
        grid=(bsz, s // tm),
        in_specs=[
            pl.BlockSpec((1, tm, d), lambda b, i: (b, i, 0)),
            pl.BlockSpec((1, N_MOD, d), mod_map),
            pl.BlockSpec((1, d), lambda b, i: (0, 0)),
            pl.BlockSpec((d, n), lambda b, i: (0, 0)),
        ],
        out_specs=[pl.BlockSpec((1, tm, b - a), lambda b_, i: (b_, i, 0)) for (a, b, _) in splits],
        out_shape=[jax.ShapeDtypeStruct((bsz, s, b - a), dt) for (a, b, _), dt in zip(splits, dtypes)],
        name=name,
    )(x, mod, g.reshape(1, d), w)


def _attn_bias(rpb):
    n_heads = rpb.shape[0]
    m = np.arange(2 * NA_KW - 1)[:, None, None]
    qc = np.arange(GRID_W)[None, :, None]
    kc = np.arange(GRID_W)[None, None, :]
    onehot = (kc - qc + NA_KW - 1 == m).astype(np.float32)
    c0 = np.clip(qc - NA_KW // 2, 0, GRID_W - NA_KW)
    col_ok = (kc >= c0) & (kc < c0 + NA_KW)
    cb = jnp.einsum("ham,mqk->haqk", rpb.astype(F32), onehot, precision=lax.Precision.HIGHEST)
    cb = jnp.where(col_ok, cb, NEG)
    masked = jnp.full((n_heads, GRID_W, GRID_W), NEG, F32)
    out = []
    for off, j0 in ((0, lambda qr: 0), (-4, lambda qr: qr), (-8, lambda qr: 4)):
        per_row = []
        for qr in range(ROWS_PER_QBLOCK):
            blocks = [cb[:, off + j - qr + NA_KH - 1] if j0(qr) <= j < j0(qr) + NA_KH else masked
                      for j in range(ROWS_PER_KWINDOW)]
            per_row.append(jnp.concatenate(blocks, axis=-1))
        out.append(jnp.concatenate(per_row, axis=1))
    return jnp.stack(out)


def _attn_kernel(q_ref, k0_ref, k1_ref, k2_ref, v0_ref, v1_ref, v2_ref, kc_ref, vc_ref, bias_ref, o_ref):
    tq = q_ref.shape[1]
    low = lax.broadcasted_iota(I32, (tq, LANES), 1) < (LANES // 2)
    for hp in range(NA_HEADS // 2):
        sl = slice(LANES * hp, LANES * (hp + 1))
        q = q_ref[0, :, sl]
        kw = jnp.concatenate([k0_ref[0, :, sl], k1_ref[0, :, sl], k2_ref[0, :, sl]], axis=0)
        vw = jnp.concatenate([v0_ref[0, :, sl], v1_ref[0, :, sl], v2_ref[0, :, sl]], axis=0)
        kcp = kc_ref[0, :, sl]
        vcp = vc_ref[0, :, sl]
        outs = []
        for hh in range(2):
            qm = jnp.where(low if hh == 0 else jnp.logical_not(low), q, jnp.zeros_like(q))
            s_loc = _nt_dot(qm, kw) + bias_ref[0, 2 * hp + hh]
            s_ctx = _nt_dot(qm, kcp)
            m = jnp.maximum(jnp.max(s_loc, axis=-1, keepdims=True), jnp.max(s_ctx, axis=-1, keepdims=True))
            e_loc = jnp.exp(s_loc - m)
            e_ctx = jnp.exp(s_ctx - m)
            den = jnp.sum(e_loc, axis=-1, keepdims=True) + jnp.sum(e_ctx, axis=-1, keepdims=True)
            o = jnp.dot(e_loc.astype(BF16), vw, preferred_element_type=F32)
            o = o + jnp.dot(e_ctx.astype(BF16), vcp, preferred_element_type=F32)
            outs.append(o / den)
        o_ref[0, :, sl] = jnp.where(low, outs[0], outs[1]).astype(o_ref.dtype)


def _attention(q, k, v, kc, vc, bias):
    bsz, s, da = q.shape
    c = kc.shape[1]
    tq = ROWS_PER_QBLOCK * GRID_W
    nb = s // tq
    assert nb >= 3, "needs at least 12 grid rows"

    def kv_map(jj):
        return lambda i, b: (b, jnp.clip(i - 1, 0, nb - 3) + jj, 0)

    def bias_map(i, b):
        return (jnp.where(i == 0, 0, jnp.where(i == nb - 1, 2, 1)), 0, 0, 0)

    kv_specs = [pl.BlockSpec((1, tq, da), kv_map(jj)) for jj in range(3)]
    return pl.pallas_call(
        _attn_kernel,
        grid=(nb, bsz),
        in_specs=[pl.BlockSpec((1, tq, da), lambda i, b: (b, i, 0))] + kv_specs + kv_specs + [
            pl.BlockSpec((1, c, da), lambda i, b: (b, 0, 0)),
            pl.BlockSpec((1, c, da), lambda i, b: (b, 0, 0)),
            pl.BlockSpec((1, NA_HEADS, tq, ROWS_PER_KWINDOW * GRID_W), bias_map),
        ],
        out_specs=pl.BlockSpec((1, tq, da), lambda i, b: (b, i, 0)),
        out_shape=jax.ShapeDtypeStruct((bsz, s, da), BF16),
        compiler_params=pltpu.CompilerParams(vmem_limit_bytes=VMEM_TABLE_LIMIT),
        name="attn",
    )(q, k, k, k, v, v, v, kc, vc, bias)


def _mix_kernel(p_ref, pprev_ref, pnext_ref, a_ref, x_ref, mod_ref, pw_ref, ps_ref, wo_ref, g2_ref,
                x1_ref, h2_ref, h8_ref, slab_scr, *, seq_len):
    tm = p_ref.shape[1]
    dp = p_ref.shape[2]
    i = pl.program_id(1)
    last = pl.num_programs(1) - 1
    p = p_ref[0]
    prev = jnp.where(i > 0, pprev_ref[0], 0.0)
    nxt = jnp.where(i < last, pnext_ref[0], 0.0)
    pe = jnp.concatenate([prev, p, nxt], axis=0)
    t = i * tm + lax.broadcasted_iota(I32, (tm, 1), 0)
    ys = []
    for g, w in enumerate(POOL_WINDOWS):
        hw = w // 2
        sl = slice(LANES * g, LANES * (g + 1))
        pg = pe[:, sl]
        acc = pg[SUBLANES - hw:SUBLANES - hw + tm]
        for d in range(-hw + 1, hw):
            acc = acc + pg[SUBLANES + d:SUBLANES + d + tm]
        cnt = (jnp.minimum(t + hw, seq_len) - jnp.maximum(t - hw, 0)).astype(F32)
        pooled = acc / cnt - p[:, sl]
        y = jnp.dot(pooled.astype(BF16), pw_ref[g], preferred_element_type=F32)
        ys.append((y * ps_ref[:, sl]).astype(BF16))
    pool_x = jnp.concatenate(ys, axis=-1)
    mixed = jnp.dot(pool_x, wo_ref[0:dp, :], preferred_element_type=F32)
    mixed = mixed + jnp.dot(a_ref[0], wo_ref[dp:, :], preferred_element_type=F32)
    m = mod_ref[0]
    x1 = x_ref[0] + m[2:3] * mixed
    x1_ref[0] = x1
    h2 = _norm_mod(x1, g2_ref[...], m[3:4], m[4:5])
    h2_ref[0] = h2.astype(h2_ref.dtype)
    _rows_to_slabs(h2, slab_scr)
    h8_ref[...] = slab_scr[...].astype(h8_ref.dtype)


def _mix(p, attn, x, mod, pool_w, pool_scale, w_out, norm2_g, tm):
    bsz, s, d = x.shape
    dp = p.shape[2]
    da = attn.shape[2]
    nblk8 = s // SUBLANES
    r = tm // SUBLANES
    return pl.pallas_call(
        functools.partial(_mix_kernel, seq_len=s),
        grid=(bsz, s // tm),
        in_specs=[
            pl.BlockSpec((1, tm, dp), lambda b, i: (b, i, 0)),
            pl.BlockSpec((1, SUBLANES, dp), lambda b, i: (b, jnp.maximum(i * r - 1, 0), 0)),
            pl.BlockSpec((1, SUBLANES, dp), lambda b, i: (b, jnp.minimum((i + 1) * r, nblk8 - 1), 0)),
            pl.BlockSpec((1, tm, da), lambda b, i: (b, i, 0)),
            pl.BlockSpec((1, tm, d), lambda b, i: (b, i, 0)),
            pl.BlockSpec((1, N_MOD, d), lambda b, i: (b, 0, 0)),
            pl.BlockSpec(pool_w.shape, lambda b, i: (0, 0, 0)),
            pl.BlockSpec((1, dp), lambda b, i: (0, 0)),
            pl.BlockSpec(w_out.shape, lambda b, i: (0, 0)),
            pl.BlockSpec((1, d), lambda b, i: (0, 0)),
        ],
        out_specs=[
            pl.BlockSpec((1, tm, d), lambda b, i: (b, i, 0)),
            pl.BlockSpec((1, tm, d), lambda b, i: (b, i, 0)),
            pl.BlockSpec((tm * SUBLANES, LANES), lambda b, i: (b * (s // tm) + i, 0)),
        ],
        out_shape=[jax.ShapeDtypeStruct((bsz, s, d), F32), jax.ShapeDtypeStruct((bsz, s, d), BF16),
                   jax.ShapeDtypeStruct((bsz * s * SUBLANES, LANES), BF16)],
        scratch_shapes=[pltpu.VMEM((tm * SUBLANES, LANES), F32)],
        name="mix",
    )(p, p, p, attn, x, mod, pool_w, pool_scale.reshape(1, dp), w_out, norm2_g.reshape(1, d))


def _argmax_tree(value, payload, lo, hi):
    if hi - lo == 1:
        return value(lo), payload(lo)
    mid = (lo + hi) // 2
    va, pa = _argmax_tree(value, payload, lo, mid)
    vb, pb = _argmax_tree(value, payload, mid, hi)
    return jnp.maximum(va, vb), jnp.where(vb > va, pb, pa).astype(I32)


def _tile(k):
    return pl.ds(k * SUBLANES, SUBLANES)


def _retrieve_kernel(h_ref, wqt_ref, keys_ref, ro_ref, g_ref, qt_scr, s_scr, sv_scr, si_scr, cand_scr, cidx_scr,
                     gt_scr, et_scr, tv_scr, te_scr):
    tq = h_ref.shape[0]
    n_hp = keys_ref.shape[0]
    assert tq == SUBLANES * LANES
    neg_inf = jnp.full((SUBLANES, LANES), -jnp.inf, F32)
    qt_scr[...] = _nt_dot(wqt_ref[...], h_ref[...])

    def first_level(hp, carry):
        q = qt_scr[pl.ds(pl.multiple_of(hp * PEER_NKEYS, PEER_NKEYS), PEER_NKEYS), :]
        s = jnp.dot(keys_ref[hp], q.astype(BF16), preferred_element_type=F32)
        _rows_to_slabs(s, s_scr)

        def extract(r, carry):
            m, idx = _argmax_tree(lambda k: s_scr[_tile(k), :], lambda k: k, 0, PEER_NKEYS)
            sv_scr[hp, r] = m
            si_scr[hp, r] = idx
            for k in range(PEER_NKEYS):
                s_scr[_tile(k), :] = jnp.where(idx == k, neg_inf, s_scr[_tile(k), :])
            return carry

        lax.fori_loop(0, PEER_TOPK, extract, 0)
        return carry

    lax.fori_loop(0, n_hp, first_level, 0)

    cells = [(i, j) for i, n in enumerate(CAND_COUNTS) for j in range(n)]

    def second_level(h, carry):
        for c, (i, j) in enumerate(cells):
            cand_scr[_tile(c), :] = sv_scr[2 * h, i] + sv_scr[2 * h + 1, j]
            cidx_scr[_tile(c), :] = si_scr[2 * h, i] * PEER_NKEYS + si_scr[2 * h + 1, j]

        def extract(r, carry):
            m, e = _argmax_tree(lambda c: cand_scr[_tile(c), :], lambda c: cidx_scr[_tile(c), :], 0, N_CAND)
            tv_scr[r] = m
            te_scr[r] = e
            for c in range(N_CAND):
                cand_scr[_tile(c), :] = jnp.where(cidx_scr[_tile(c), :] == e, neg_inf, cand_scr[_tile(c), :])
            return carry

        lax.fori_loop(0, PEER_TOPK, extract, 0)
        es = [jnp.exp(tv_scr[r] - tv_scr[0]) for r in range(PEER_TOPK)]
        den = es[0]
        for e in es[1:]:
            den = den + e
        for r in range(PEER_TOPK):
            row = pl.ds(pl.multiple_of((h * PEER_TOPK + r) * SUBLANES, SUBLANES), SUBLANES)
            gt_scr[row, :] = es[r] / den
            et_scr[row, :] = te_scr[r] * SLAB + (TABLE_PAD if r % 2 == 0 else TABLE_PAD - SLAB)
        return carry

    lax.fori_loop(0, n_hp // 2, second_level, 0)

    g_rows = _slabs_to_rows(gt_scr)
    ro_rows = _slabs_to_rows(et_scr)
    for c in range(tq // LANES):
        g_ref[c * LANES:(c + 1) * LANES, :] = g_rows[:, c * LANES:(c + 1) * LANES].T
        ro_ref[c] = ro_rows[:, c * LANES:(c + 1) * LANES]


def _retrieve(h2, wqt, keys, tq):
    n, d = h2.shape
    nq = wqt.shape[0]
    n_hp = keys.shape[0]
    n_picks = PEER_HEADS * PEER_TOPK
    return pl.pallas_call(
        _retrieve_kernel,
        grid=(n // tq,),
        in_specs=[
            pl.BlockSpec((tq, d), lambda i: (i, 0)),
            pl.BlockSpec((nq, d), lambda i: (0, 0)),
            pl.BlockSpec(keys.shape, lambda i: (0, 0, 0)),
        ],
        out_specs=[pl.BlockSpec((tq // LANES, n_picks, LANES), lambda i: (i, 0, 0)),
                   pl.BlockSpec((tq, n_picks), lambda i: (i, 0))],
        out_shape=[jax.ShapeDtypeStruct((n // LANES, n_picks, LANES), I32), jax.ShapeDtypeStruct((n, n_picks), F32)],
        scratch_shapes=[
            pltpu.VMEM((nq, tq), F32),
            pltpu.VMEM((PEER_NKEYS * SUBLANES, LANES), F32),
            pltpu.VMEM((n_hp, PEER_TOPK, SUBLANES, LANES), F32),
            pltpu.VMEM((n_hp, PEER_TOPK, SUBLANES, LANES), I32),
            pltpu.VMEM((N_CAND * SUBLANES, LANES), F32),
            pltpu.VMEM((N_CAND * SUBLANES, LANES), I32),
            pltpu.VMEM((n_picks * SUBLANES, LANES), F32),
            pltpu.VMEM((n_picks * SUBLANES, LANES), I32),
            pltpu.VMEM((PEER_TOPK, SUBLANES, LANES), F32),
            pltpu.VMEM((PEER_TOPK, SUBLANES, LANES), I32),
        ],
        compiler_params=pltpu.CompilerParams(vmem_limit_bytes=VMEM_TABLE_LIMIT),
        name="retrieve",
    )(h2, wqt, keys)


def _table_scratch(n_experts, d):
    assert d == SUBLANES * LANES and n_experts % TABLE_FILL_ROWS == 0
    return [
        pltpu.VMEM((n_experts * SLAB + 2 * TABLE_PAD, LANES), U32),
        pltpu.VMEM((2, TABLE_FILL_ROWS, d), F32),
        pltpu.VMEM((TABLE_FILL_ROWS * SUBLANES, LANES), F32),
        pltpu.SemaphoreType.DMA((2,)),
    ]


def _fill_table(tabs_hbm, layer, tab_scr, buf, slab_scr, sem):
    n_experts = tabs_hbm.shape[1]
    te = buf.shape[1]
    n_chunks = n_experts // te
    pad = jnp.zeros((TABLE_PAD, LANES), U32)
    tab_scr[0:TABLE_PAD, :] = pad
    tab_scr[TABLE_PAD + n_experts * SLAB:, :] = pad

    def chunk_copy(c, slot):
        return pltpu.make_async_copy(tabs_hbm.at[layer, pl.ds(c * te, te)], buf.at[slot], sem.at[slot])

    chunk_copy(0, 0).start()

    def chunk(c, carry):
        slot = c % 2

        @pl.when(c + 1 < n_chunks)
        def _():
            chunk_copy(c + 1, 1 - slot).start()

        chunk_copy(c, slot).wait()
        _rows_to_slabs(buf[slot], slab_scr)
        rows = pl.ds(pl.multiple_of(TABLE_PAD + c * (te * SLAB), SUBLANES), te * SLAB)
        tab_scr[rows, :] = pltpu.bitcast(slab_scr[...].astype(BF16), U32)
        return carry

    lax.fori_loop(0, n_chunks, chunk, 0)


def _gather_order(n_picks):
    per_ref = n_picks // RO_SPLIT
    return [j * per_ref + m + b for m in range(0, per_ref, 2) for j in range(RO_SPLIT) for b in range(2)]


def _expand_matrix(n_picks):
    rep = np.zeros((n_picks, n_picks * SUBLANES), np.float32)
    for pos, k in enumerate(_gather_order(n_picks)):
        rep[k, pos * SUBLANES:(pos + 1) * SUBLANES] = 1.0
    return jnp.asarray(rep, BF16)


def _gather_slabs(ro_wins, tab_ref, tl, n_picks):
    per_ref = n_picks // RO_SPLIT
    order = _gather_order(n_picks)
    halves = []
    for pos in range(0, n_picks, 2):
        k_lo, k_hi = order[pos], order[pos + 1]
        lo = tab_ref[pl.ds(ro_wins[k_lo // per_ref][k_lo % per_ref, tl], SUBLANES), :]
        hi = tab_ref[pl.ds(ro_wins[k_hi // per_ref][k_hi % per_ref, tl], SUBLANES), :]
        halves += [lo[:SLAB], hi[SLAB:]]
    return pltpu.bitcast(jnp.concatenate(halves, axis=0), BF16)


def _token_windows(ro_refs, tg):
    toks = pl.ds(pl.multiple_of(tg * SUBLANES, SUBLANES), SUBLANES)
    return [r.at[:, toks] for r in ro_refs]


def _peer_u_kernel(*refs, layer):
    ro_refs = refs[:RO_SPLIT]
    g_ref, h8_ref, tabs_hbm, sum8_ref, w_ref, a8_scr, tab_ref, buf, slab_scr, sem = refs[RO_SPLIT:]
    tt, n_picks = g_ref.shape
    eye = (lax.broadcasted_iota(I32, (SUBLANES, n_picks * SUBLANES), 1) % SUBLANES
           == lax.broadcasted_iota(I32, (SUBLANES, n_picks * SUBLANES), 0))

    @pl.when(pl.program_id(0) == 0)
    def _():
        _fill_table(tabs_hbm, layer, tab_ref, buf, slab_scr, sem)

    def token_group(tg, carry):
        rows = []
        ro_wins = _token_windows(ro_refs, tg)
        for tl in range(SUBLANES):
            t = tg * SUBLANES + tl
            bm = _gather_slabs(ro_wins, tab_ref, tl, n_picks)
            x8 = h8_ref[pl.ds(pl.multiple_of(t * SUBLANES, SUBLANES), SUBLANES), :]
            r = _nt_dot(x8, bm)
            rows.append(jnp.sum(jnp.where(eye, r, 0.0), axis=0, keepdims=True))
        a8_scr[pl.ds(pl.multiple_of(tg * SUBLANES, SUBLANES), SUBLANES), :] = jnp.concatenate(rows, axis=0)
        return carry

    lax.fori_loop(0, tt // SUBLANES, token_group, 0)
    a8 = a8_scr[...]
    hi = a8.astype(BF16)
    lo = (a8 - hi.astype(F32)).astype(BF16)
    a = jnp.dot(hi, sum8_ref[...], preferred_element_type=F32) + jnp.dot(lo, sum8_ref[...], preferred_element_type=F32)
    gelu = 0.5 * a * (1.0 + lax.erf(a * np.float32(2.0 ** -0.5)))
    w_ref[...] = g_ref[...] * gelu


def _offset_specs(tt, n_picks):
    per_ref = n_picks // RO_SPLIT
    return [pl.BlockSpec((None, per_ref, tt), lambda i, j=j: (i, j, 0), memory_space=pltpu.SMEM)
            for j in range(RO_SPLIT)]


def _peer_u(ro, g, h8, tabs, layer, sum8):
    nt, n_picks, tt = ro.shape
    ro_parts = [ro] * RO_SPLIT
    return pl.pallas_call(
        functools.partial(_peer_u_kernel, layer=layer),
        grid=(nt,),
        in_specs=_offset_specs(tt, n_picks) + [
            pl.BlockSpec((tt, n_picks), lambda i: (i, 0)),
            pl.BlockSpec((tt * SUBLANES, LANES), lambda i: (i, 0)),
            pl.BlockSpec(memory_space=pl.ANY),
            pl.BlockSpec(sum8.shape, lambda i: (0, 0)),
        ],
        out_specs=pl.BlockSpec((tt, n_picks), lambda i: (i, 0)),
        out_shape=jax.ShapeDtypeStruct((nt * tt, n_picks), F32),
        scratch_shapes=[pltpu.VMEM((tt, n_picks * SUBLANES), F32)] + _table_scratch(*tabs.shape[1:]),
        compiler_params=pltpu.CompilerParams(dimension_semantics=("arbitrary",),
                                             vmem_limit_bytes=VMEM_TABLE_LIMIT),
        name="peer_u",
    )(*ro_parts, g, h8, tabs, sum8)


def _peer_v_kernel(*refs, layer):
    ro_refs = refs[:RO_SPLIT]
    w_ref, tabs_hbm, rep8_ref, o8_ref, whi_scr, wlo_scr, tab_ref, buf, slab_scr, sem = refs[RO_SPLIT:]
    tt, n_picks = w_ref.shape

    @pl.when(pl.program_id(0) == 0)
    def _():
        _fill_table(tabs_hbm, layer, tab_ref, buf, slab_scr, sem)

    w = w_ref[...]
    hi = w.astype(BF16)
    lo = (w - hi.astype(F32)).astype(BF16)
    whi_scr[...] = jnp.dot(hi, rep8_ref[...], preferred_element_type=F32)
    wlo_scr[...] = jnp.dot(lo, rep8_ref[...], preferred_element_type=F32)
    eye = (lax.broadcasted_iota(I32, (SUBLANES, n_picks * SUBLANES), 1) % SUBLANES
           == lax.broadcasted_iota(I32, (SUBLANES, n_picks * SUBLANES), 0))

    def token_group(tg, carry):
        ro_wins = _token_windows(ro_refs, tg)
        for tl in range(SUBLANES):
            t = tg * SUBLANES + tl
            bm = _gather_slabs(ro_wins, tab_ref, tl, n_picks)
            w_hi = jnp.where(eye, whi_scr[pl.ds(t, 1), :], 0.0).astype(BF16)
            w_lo = jnp.where(eye, wlo_scr[pl.ds(t, 1), :], 0.0).astype(BF16)
            r = jnp.dot(jnp.concatenate([w_hi, w_lo], axis=0), bm, preferred_element_type=F32)
            o8_ref[pl.ds(pl.multiple_of(t * SUBLANES, SUBLANES), SUBLANES), :] = r[0:SUBLANES] + r[SUBLANES:]
        return carry

    lax.fori_loop(0, tt // SUBLANES, token_group, 0)


def _peer_v(ro, w, tabs, layer, rep8):
    nt, n_picks, tt = ro.shape
    ro_parts = [ro] * RO_SPLIT
    return pl.pallas_call(
        functools.partial(_peer_v_kernel, layer=layer),
        grid=(nt,),
        in_specs=_offset_specs(tt, n_picks) + [
            pl.BlockSpec((tt, n_picks), lambda i: (i, 0)),
            pl.BlockSpec(memory_space=pl.ANY),
            pl.BlockSpec(rep8.shape, lambda i: (0, 0)),
        ],
        out_specs=pl.BlockSpec((tt * SUBLANES, LANES), lambda i: (i, 0)),
        out_shape=jax.ShapeDtypeStruct((nt * tt * SUBLANES, LANES), F32),
        scratch_shapes=[pltpu.VMEM((tt, n_picks * SUBLANES), F32), pltpu.VMEM((tt, n_picks * SUBLANES), F32)]
        + _table_scratch(*tabs.shape[1:]),
        compiler_params=pltpu.CompilerParams(dimension_semantics=("arbitrary",),
                                             vmem_limit_bytes=VMEM_TABLE_LIMIT),
        name="peer_v",
    )(*ro_parts, w, tabs, rep8)


def _final_kernel(x_ref, o8_ref, mod_ref, g_ref, y_ref):
    m = mod_ref[0]
    x = x_ref[0] + m[5:6] * _slabs_to_rows(o8_ref)
    ms = jnp.mean(x * x, axis=-1, keepdims=True)
    y_ref[0] = x * lax.rsqrt(ms + EPS) * g_ref[...]


def _final(x1, o8, mod, g, tm):
    bsz, s, d = x1.shape
    spec = pl.BlockSpec((1, tm, d), lambda b, i: (b, i, 0))
    return pl.pallas_call(
        _final_kernel,
        grid=(bsz, s // tm),
        in_specs=[spec, pl.BlockSpec((tm * SUBLANES, LANES), lambda b, i: (b * (s // tm) + i, 0)),
                  pl.BlockSpec((1, N_MOD, d), lambda b, i: (b, 0, 0)),
                  pl.BlockSpec((1, d), lambda b, i: (0, 0))],
        out_specs=spec,
        out_shape=jax.ShapeDtypeStruct((bsz, s, d), F32),
        name="final",
    )(x1, o8, mod, g.reshape(1, d))


def _layer(layer, x, ctx, mods, norm1_g, w_in, pool_w, pool_scale, rpb, w_out, norm2_g, wq, keys, u_tabs, v_tabs):
    bsz, s, d = x.shape
    d_in = w_in.shape[1]
    d_pool = pool_scale.shape[0]
    d_attn = (d_in - d_pool) // 3
    mod_x = mods[:bsz].reshape(bsz, N_MOD, d)
    mod_c = mods[bsz:bsz + 1].reshape(1, N_MOD, d)
    w_in_b = w_in.astype(BF16)
    scale = float((d_attn // NA_HEADS) ** -0.5)

    tm = min(512, s)
    p, q, k, vv = _in_proj(
        x, mod_x, norm1_g, w_in_b,
        splits=((0, d_pool, 1.0), (d_pool, d_pool + d_attn, scale),
                (d_pool + d_attn, d_pool + 2 * d_attn, 1.0), (d_pool + 2 * d_attn, d_in, 1.0)),
        dtypes=(F32, BF16, BF16, BF16), tm=tm, mod_per_batch=True, name="in_proj")
    kc, vc = _in_proj(
        ctx, mod_c, norm1_g, w_in_b[:, d_pool + d_attn:],
        splits=((0, d_attn, 1.0), (d_attn, 2 * d_attn, 1.0)),
        dtypes=(BF16, BF16), tm=ctx.shape[1], mod_per_batch=False, name="ctx_proj")

    attn = _attention(q, k, vv, kc, vc, _attn_bias(rpb))
    x1, h2, h8 = _mix(p, attn, x, mod_x, pool_w.astype(BF16), pool_scale, w_out.astype(BF16), norm2_g, tm)

    n = bsz * s
    ro, g = _retrieve(h2.reshape(n, d), wq.T.astype(BF16),
                      keys.reshape(-1, PEER_NKEYS, keys.shape[-1]).astype(BF16), tq=SUBLANES * LANES)
    n_picks = PEER_HEADS * PEER_TOPK
    rep8 = _expand_matrix(n_picks)
    w = _peer_u(ro, g, h8, u_tabs, layer, rep8.T)
    o8 = _peer_v(ro, w, v_tabs, layer, rep8)
    return x1, o8, mod_x, tm


def kernel(x, c, ctx, c_ctx, ada_w, ada_b, norm1_g, w_in, pool_w, pool_scale, na_rpb, w_out, norm2_g, peer_wq,
           peer_keys, peer_u, peer_v, final_g):
    depth = ada_w.shape[0]
    assert depth == 1, "context-stream update of deeper stacks is not implemented"
    bsz, s, d = x.shape
    rows_c = -(-(bsz + 1) // SUBLANES) * SUBLANES
    cc = jnp.concatenate([c, c_ctx[None, :], jnp.zeros((rows_c - bsz - 1, d), F32)], axis=0)
    mods = _ada(cc, ada_w[0], ada_b[0])
    x1, o8, mod_x, tm = _layer(0, x, ctx, mods, norm1_g[0], w_in[0], pool_w[0], pool_scale[0], na_rpb[0], w_out[0],
                               norm2_g[0], peer_wq[0], peer_keys[0], peer_u, peer_v)
    return _final(x1, o8, mod_x, final_g, tm)
```

```python
import functools

import jax
import jax.numpy as jnp
import numpy as np
from jax import lax
from jax.experimental import pallas as pl
from jax.experimental.pallas import tpu as pltpu

F32 = jnp.float32
BF16 = jnp.bfloat16
I32 = jnp.int32
U32 = jnp.uint32

EPS = 1e-6
GRID_W = 64
POOL_WINDOWS = (2, 4, 8, 16)
NA_HEADS = 8
NA_KH = 8
NA_KW = 16
N_MOD = 6
PEER_HEADS = 8
PEER_NKEYS = 128
PEER_TOPK = 16

LANES = 128
SUBLANES = 8
ROWS_PER_QBLOCK = 4
ROWS_PER_KWINDOW = 12
NEG = -1e30
SLAB = 4
TABLE_PAD = 8
RO_SPLIT = 8
TABLE_FILL_ROWS = 256
TOKENS_PER_ITER = 64
VMEM_TABLE_LIMIT = 48 * 1024 * 1024

CAND_COUNTS = tuple(PEER_TOPK // (i + 1) for i in range(PEER_TOPK))
N_CAND = sum(CAND_COUNTS)


def _nt_dot(a, b):
    return lax.dot_general(a, b, (((1,), (1,)), ((), ())), preferred_element_type=F32)


def _ada_kernel(c_ref, w_ref, b_ref, o_ref):
    s = jax.nn.silu(c_ref[...])
    o_ref[...] = jnp.dot(s.astype(BF16), w_ref[...].astype(BF16), preferred_element_type=F32) + b_ref[...]


def _ada(cc, w, b):
    m, d = cc.shape
    n = w.shape[1]
    return pl.pallas_call(
        _ada_kernel,
        grid=(n // d,),
        in_specs=[
            pl.BlockSpec((m, d), lambda j: (0, 0)),
            pl.BlockSpec((d, d), lambda j: (0, j)),
            pl.BlockSpec((1, d), lambda j: (0, j)),
        ],
        out_specs=pl.BlockSpec((m, d), lambda j: (0, j)),
        out_shape=jax.ShapeDtypeStruct((m, n), F32),
        name="ada",
    )(cc, w, b.reshape(1, n))


def _rows_to_slabs(x, slab_ref):
    t = x.shape[0]
    for j in range(SUBLANES):
        slab_ref[pl.ds(j, t, stride=SUBLANES), :] = x[:, LANES * j:LANES * (j + 1)]


def _slabs_to_rows(slab_ref):
    t = slab_ref.shape[0] // SUBLANES
    return jnp.concatenate([slab_ref[pl.ds(j, t, stride=SUBLANES), :] for j in range(SUBLANES)], axis=-1)


def _norm_mod(x, g, shift, scale):
    ms = jnp.mean(x * x, axis=-1, keepdims=True)
    y = x * lax.rsqrt(ms + EPS) * g
    return y * (1.0 + scale) + shift


def _in_proj_kernel(x_ref, mod_ref, g_ref, w_ref, *out_refs, splits):
    m = mod_ref[0]
    h = _norm_mod(x_ref[0], g_ref[...], m[0:1], m[1:2])
    z = jnp.dot(h.astype(BF16), w_ref[...], preferred_element_type=F32)
    for o_ref, (a, b, mult) in zip(out_refs, splits):
        v = z[:, a:b]
        if mult != 1.0:
            v = v * mult
        o_ref[0] = v.astype(o_ref.dtype)


def _in_proj(x, mod, g, w, splits, dtypes, tm, mod_per_batch, name):
    bsz, s, d = x.shape
    n = w.shape[1]
    mod_map = (lambda b, i: (b, 0, 0)) if mod_per_batch else (lambda b, i: (0, 0, 0))
    return pl.pallas_call(
        functools.partial(_in_proj_kernel, splits=splits),
        grid=(bsz, s // tm),
        in_specs=[
            pl.BlockSpec((1, tm, d), lambda b, i: (b, i, 0)),
            pl.BlockSpec((1, N_MOD, d), mod_map),
            pl.BlockSpec((1, d), lambda b, i: (0, 0)),
            pl.BlockSpec((d, n), lambda b, i: (0, 0)),
        ],
        out_specs=[pl.BlockSpec((1, tm, b - a), lambda b_, i: (b_, i, 0)) for (a, b, _) in splits],
        out_shape=[jax.ShapeDtypeStruct((bsz, s, b - a), dt) for (a, b, _), dt in zip(splits, dtypes)],
        name=name,
    )(x, mod, g.reshape(1, d), w)


def _attn_bias(rpb):
    n_heads = rpb.shape[0]
    m = np.arange(2 * NA_KW - 1)[:, None, None]
    qc = np.arange(GRID_W)[None, :, None]
    kc = np.arange(GRID_W)[None, None, :]
    onehot = (kc - qc + NA_KW - 1 == m).astype(np.float32)
    c0 = np.clip(qc - NA_KW // 2, 0, GRID_W - NA_KW)
    col_ok = (kc >= c0) & (kc < c0 + NA_KW)
    cb = jnp.einsum("ham,mqk->haqk", rpb.astype(F32), onehot, precision=lax.Precision.HIGHEST)
    cb = jnp.where(col_ok, cb, NEG)
    masked = jnp.full((n_heads, GRID_W, GRID_W), NEG, F32)
    out = []
    for off, j0 in ((0, lambda qr: 0), (-4, lambda qr: qr), (-8, lambda qr: 4)):
        per_row = []
        for qr in range(ROWS_PER_QBLOCK):
            blocks = [cb[:, off + j - qr + NA_KH - 1] if j0(qr) <= j < j0(qr) + NA_KH else masked
                      for j in range(ROWS_PER_KWINDOW)]
            per_row.append(jnp.concatenate(blocks, axis=-1))
        out.append(jnp.concatenate(per_row, axis=1))
    return jnp.stack(out)


def _attn_kernel(q_ref, k0_ref, k1_ref, k2_ref, v0_ref, v1_ref, v2_ref, kc_ref, vc_ref, bias_ref, o_ref):
    tq = q_ref.shape[1]
    low = lax.broadcasted_iota(I32, (tq, LANES), 1) < (LANES // 2)
    for hp in range(NA_HEADS // 2):
        sl = slice(LANES * hp, LANES * (hp + 1))
        q = q_ref[0, :, sl]
        kw = jnp.concatenate([k0_ref[0, :, sl], k1_ref[0, :, sl], k2_ref[0, :, sl]], axis=0)
        vw = jnp.concatenate([v0_ref[0, :, sl], v1_ref[0, :, sl], v2_ref[0, :, sl]], axis=0)
        kcp = kc_ref[0, :, sl]
        vcp = vc_ref[0, :, sl]
        outs = []
        for hh in range(2):
            qm = jnp.where(low if hh == 0 else jnp.logical_not(low), q, jnp.zeros_like(q))
            s_loc = _nt_dot(qm, kw) + bias_ref[0, 2 * hp + hh]
            s_ctx = _nt_dot(qm, kcp)
            m = jnp.maximum(jnp.max(s_loc, axis=-1, keepdims=True), jnp.max(s_ctx, axis=-1, keepdims=True))
            e_loc = jnp.exp(s_loc - m)
            e_ctx = jnp.exp(s_ctx - m)
            den = jnp.sum(e_loc, axis=-1, keepdims=True) + jnp.sum(e_ctx, axis=-1, keepdims=True)
            o = jnp.dot(e_loc.astype(BF16), vw, preferred_element_type=F32)
            o = o + jnp.dot(e_ctx.astype(BF16), vcp, preferred_element_type=F32)
            outs.append(o / den)
        o_ref[0, :, sl] = jnp.where(low, outs[0], outs[1]).astype(o_ref.dtype)


def _attention(q, k, v, kc, vc, bias):
    bsz, s, da = q.shape
    c = kc.shape[1]
    tq = ROWS_PER_QBLOCK * GRID_W
    nb = s // tq
    assert nb >= 3, "needs at least 12 grid rows"

    def kv_map(jj):
        return lambda i, b: (b, jnp.clip(i - 1, 0, nb - 3) + jj, 0)

    def bias_map(i, b):
        return (jnp.where(i == 0, 0, jnp.where(i == nb - 1, 2, 1)), 0, 0, 0)

    kv_specs = [pl.BlockSpec((1, tq, da), kv_map(jj)) for jj in range(3)]
    return pl.pallas_call(
        _attn_kernel,
        grid=(nb, bsz),
        in_specs=[pl.BlockSpec((1, tq, da), lambda i, b: (b, i, 0))] + kv_specs + kv_specs + [
            pl.BlockSpec((1, c, da), lambda i, b: (b, 0, 0)),
            pl.BlockSpec((1, c, da), lambda i, b: (b, 0, 0)),
            pl.BlockSpec((1, NA_HEADS, tq, ROWS_PER_KWINDOW * GRID_W), bias_map),
        ],
        out_specs=pl.BlockSpec((1, tq, da), lambda i, b: (b, i, 0)),
        out_shape=jax.ShapeDtypeStruct((bsz, s, da), BF16),
        compiler_params=pltpu.CompilerParams(vmem_limit_bytes=VMEM_TABLE_LIMIT),
        name="attn",
    )(q, k, k, k, v, v, v, kc, vc, bias)


def _mix_kernel(p_ref, pprev_ref, pnext_ref, a_ref, x_ref, mod_ref, pw_ref, ps_ref, wo_ref, g2_ref,
                x1_ref, h2_ref, h8_ref, slab_scr, *, seq_len):
    tm = p_ref.shape[1]
    dp = p_ref.shape[2]
    i = pl.program_id(1)
    last = pl.num_programs(1) - 1
    p = p_ref[0]
    prev = jnp.where(i > 0, pprev_ref[0], 0.0)
    nxt = jnp.where(i < last, pnext_ref[0], 0.0)
    pe = jnp.concatenate([prev, p, nxt], axis=0)
    t = i * tm + lax.broadcasted_iota(I32, (tm, 1), 0)
    ys = []
    for g, w in enumerate(POOL_WINDOWS):
        hw = w // 2
        sl = slice(LANES * g, LANES * (g + 1))
        pg = pe[:, sl]
        acc = pg[SUBLANES - hw:SUBLANES - hw + tm]
        for d in range(-hw + 1, hw):
            acc = acc + pg[SUBLANES + d:SUBLANES + d + tm]
        cnt = (jnp.minimum(t + hw, seq_len) - jnp.maximum(t - hw, 0)).astype(F32)
        pooled = acc / cnt - p[:, sl]
        y = jnp.dot(pooled.astype(BF16), pw_ref[g], preferred_element_type=F32)
        ys.append((y * ps_ref[:, sl]).astype(BF16))
    pool_x = jnp.concatenate(ys, axis=-1)
    mixed = jnp.dot(pool_x, wo_ref[0:dp, :], preferred_element_type=F32)
    mixed = mixed + jnp.dot(a_ref[0], wo_ref[dp:, :], preferred_element_type=F32)
    m = mod_ref[0]
    x1 = x_ref[0] + m[2:3] * mixed
    x1_ref[0] = x1
    h2 = _norm_mod(x1, g2_ref[...], m[3:4], m[4:5])
    h2_ref[0] = h2.astype(h2_ref.dtype)
    _rows_to_slabs(h2, slab_scr)
    h8_ref[...] = slab_scr[...].astype(h8_ref.dtype)


def _mix(p, attn, x, mod, pool_w, pool_scale, w_out, norm2_g, tm):
    bsz, s, d = x.shape
    dp = p.shape[2]
    da = attn.shape[2]
    nblk8 = s // SUBLANES
    r = tm // SUBLANES
    return pl.pallas_call(
        functools.partial(_mix_kernel, seq_len=s),
        grid=(bsz, s // tm),
        in_specs=[
            pl.BlockSpec((1, tm, dp), lambda b, i: (b, i, 0)),
            pl.BlockSpec((1, SUBLANES, dp), lambda b, i: (b, jnp.maximum(i * r - 1, 0), 0)),
            pl.BlockSpec((1, SUBLANES, dp), lambda b, i: (b, jnp.minimum((i + 1) * r, nblk8 - 1), 0)),
            pl.BlockSpec((1, tm, da), lambda b, i: (b, i, 0)),
            pl.BlockSpec((1, tm, d), lambda b, i: (b, i, 0)),
            pl.BlockSpec((1, N_MOD, d), lambda b, i: (b, 0, 0)),
            pl.BlockSpec(pool_w.shape, lambda b, i: (0, 0, 0)),
            pl.BlockSpec((1, dp), lambda b, i: (0, 0)),
            pl.BlockSpec(w_out.shape, lambda b, i: (0, 0)),
            pl.BlockSpec((1, d), lambda b, i: (0, 0)),
        ],
        out_specs=[
            pl.BlockSpec((1, tm, d), lambda b, i: (b, i, 0)),
            pl.BlockSpec((1, tm, d), lambda b, i: (b, i, 0)),
            pl.BlockSpec((tm * SUBLANES, LANES), lambda b, i: (b * (s // tm) + i, 0)),
        ],
        out_shape=[jax.ShapeDtypeStruct((bsz, s, d), F32), jax.ShapeDtypeStruct((bsz, s, d), BF16),
                   jax.ShapeDtypeStruct((bsz * s * SUBLANES, LANES), BF16)],
        scratch_shapes=[pltpu.VMEM((tm * SUBLANES, LANES), F32)],
        name="mix",
    )(p, p, p, attn, x, mod, pool_w, pool_scale.reshape(1, dp), w_out, norm2_g.reshape(1, d))


def _argmax_tree(value, payload, lo, hi):
    if hi - lo == 1:
        return value(lo), payload(lo)
    mid = (lo + hi) // 2
    va, pa = _argmax_tree(value, payload, lo, mid)
    vb, pb = _argmax_tree(value, payload, mid, hi)
    return jnp.maximum(va, vb), jnp.where(vb > va, pb, pa).astype(I32)


def _tile(k):
    return pl.ds(k * SUBLANES, SUBLANES)


def _retrieve_kernel(h_ref, wqt_ref, keys_ref, ro_ref, g_ref, qt_scr, s_scr, sv_scr, si_scr, cand_scr, cidx_scr,
                     gt_scr, et_scr, tv_scr, te_scr):
    tq = h_ref.shape[0]
    n_hp = keys_ref.shape[0]
    assert tq == SUBLANES * LANES
    neg_inf = jnp.full((SUBLANES, LANES), -jnp.inf, F32)
    qt_scr[...] = _nt_dot(wqt_ref[...], h_ref[...])

    def first_level(hp, carry):
        q = qt_scr[pl.ds(pl.multiple_of(hp * PEER_NKEYS, PEER_NKEYS), PEER_NKEYS), :]
        s = jnp.dot(keys_ref[hp], q.astype(BF16), preferred_element_type=F32)
        _rows_to_slabs(s, s_scr)

        def extract(r, carry):
            m, idx = _argmax_tree(lambda k: s_scr[_tile(k), :], lambda k: k, 0, PEER_NKEYS)
            sv_scr[hp, r] = m
            si_scr[hp, r] = idx
            for k in range(PEER_NKEYS):
                s_scr[_tile(k), :] = jnp.where(idx == k, neg_inf, s_scr[_tile(k), :])
            return carry

        lax.fori_loop(0, PEER_TOPK, extract, 0)
        return carry

    lax.fori_loop(0, n_hp, first_level, 0)

    cells = [(i, j) for i, n in enumerate(CAND_COUNTS) for j in range(n)]

    def second_level(h, carry):
        for c, (i, j) in enumerate(cells):
            cand_scr[_tile(c), :] = sv_scr[2 * h, i] + sv_scr[2 * h + 1, j]
            cidx_scr[_tile(c), :] = si_scr[2 * h, i] * PEER_NKEYS + si_scr[2 * h + 1, j]

        def extract(r, carry):
            m, e = _argmax_tree(lambda c: cand_scr[_tile(c), :], lambda c: cidx_scr[_tile(c), :], 0, N_CAND)
            tv_scr[r] = m
            te_scr[r] = e
            for c in range(N_CAND):
                cand_scr[_tile(c), :] = jnp.where(cidx_scr[_tile(c), :] == e, neg_inf, cand_scr[_tile(c), :])
            return carry

        lax.fori_loop(0, PEER_TOPK, extract, 0)
        es = [jnp.exp(tv_scr[r] - tv_scr[0]) for r in range(PEER_TOPK)]
        den = es[0]
        for e in es[1:]:
            den = den + e
        for r in range(PEER_TOPK):
            row = pl.ds(pl.multiple_of((h * PEER_TOPK + r) * SUBLANES, SUBLANES), SUBLANES)
            gt_scr[row, :] = es[r] / den
            et_scr[row, :] = te_scr[r] * SLAB + (TABLE_PAD if r % 2 == 0 else TABLE_PAD - SLAB)
        return carry

    lax.fori_loop(0, n_hp // 2, second_level, 0)

    g_rows = _slabs_to_rows(gt_scr)
    ro_rows = _slabs_to_rows(et_scr)
    for c in range(tq // LANES):
        g_ref[c * LANES:(c + 1) * LANES, :] = g_rows[:, c * LANES:(c + 1) * LANES].T
        ro_ref[c] = ro_rows[:, c * LANES:(c + 1) * LANES]


def _retrieve(h2, wqt, keys, tq):
    n, d = h2.shape
    nq = wqt.shape[0]
    n_hp = keys.shape[0]
    n_picks = PEER_HEADS * PEER_TOPK
    return pl.pallas_call(
        _retrieve_kernel,
        grid=(n // tq,),
        in_specs=[
            pl.BlockSpec((tq, d), lambda i: (i, 0)),
            pl.BlockSpec((nq, d), lambda i: (0, 0)),
            pl.BlockSpec(keys.shape, lambda i: (0, 0, 0)),
        ],
        out_specs=[pl.BlockSpec((tq // LANES, n_picks, LANES), lambda i: (i, 0, 0)),
                   pl.BlockSpec((tq, n_picks), lambda i: (i, 0))],
        out_shape=[jax.ShapeDtypeStruct((n // LANES, n_picks, LANES), I32), jax.ShapeDtypeStruct((n, n_picks), F32)],
        scratch_shapes=[
            pltpu.VMEM((nq, tq), F32),
            pltpu.VMEM((PEER_NKEYS * SUBLANES, LANES), F32),
            pltpu.VMEM((n_hp, PEER_TOPK, SUBLANES, LANES), F32),
            pltpu.VMEM((n_hp, PEER_TOPK, SUBLANES, LANES), I32),
            pltpu.VMEM((N_CAND * SUBLANES, LANES), F32),
            pltpu.VMEM((N_CAND * SUBLANES, LANES), I32),
            pltpu.VMEM((n_picks * SUBLANES, LANES), F32),
            pltpu.VMEM((n_picks * SUBLANES, LANES), I32),
            pltpu.VMEM((PEER_TOPK, SUBLANES, LANES), F32),
            pltpu.VMEM((PEER_TOPK, SUBLANES, LANES), I32),
        ],
        compiler_params=pltpu.CompilerParams(vmem_limit_bytes=VMEM_TABLE_LIMIT),
        name="retrieve",
    )(h2, wqt, keys)


def _table_scratch(n_experts, d):
    assert d == SUBLANES * LANES and n_experts % TABLE_FILL_ROWS == 0
    return [
        pltpu.VMEM((n_experts * SLAB + 2 * TABLE_PAD, LANES), U32),
        pltpu.VMEM((2, TABLE_FILL_ROWS, d), F32),
        pltpu.VMEM((TABLE_FILL_ROWS * SUBLANES, LANES), F32),
        pltpu.SemaphoreType.DMA((2,)),
    ]


def _fill_table(tabs_hbm, layer, tab_scr, buf, slab_scr, sem):
    n_experts = tabs_hbm.shape[1]
    te = buf.shape[1]
    n_chunks = n_experts // te
    pad = jnp.zeros((TABLE_PAD, LANES), U32)
    tab_scr[0:TABLE_PAD, :] = pad
    tab_scr[TABLE_PAD + n_experts * SLAB:, :] = pad

    def chunk_copy(c, slot):
        return pltpu.make_async_copy(tabs_hbm.at[layer, pl.ds(c * te, te)], buf.at[slot], sem.at[slot])

    chunk_copy(0, 0).start()

    def chunk(c, carry):
        slot = c % 2

        @pl.when(c + 1 < n_chunks)
        def _():
            chunk_copy(c + 1, 1 - slot).start()

        chunk_copy(c, slot).wait()
        _rows_to_slabs(buf[slot], slab_scr)
        rows = pl.ds(pl.multiple_of(TABLE_PAD + c * (te * SLAB), SUBLANES), te * SLAB)
        tab_scr[rows, :] = pltpu.bitcast(slab_scr[...].astype(BF16), U32)
        return carry

    lax.fori_loop(0, n_chunks, chunk, 0)


def _gather_order(n_picks):
    per_ref = n_picks // RO_SPLIT
    return [j * per_ref + m + b for m in range(0, per_ref, 2) for j in range(RO_SPLIT) for b in range(2)]


def _expand_matrix(n_picks):
    rep = np.zeros((n_picks, n_picks * SUBLANES), np.float32)
    for pos, k in enumerate(_gather_order(n_picks)):
        rep[k, pos * SUBLANES:(pos + 1) * SUBLANES] = 1.0
    return jnp.asarray(rep, BF16)


def _gather_slabs(ro_wins, tab_ref, tl, n_picks):
    per_ref = n_picks // RO_SPLIT
    order = _gather_order(n_picks)
    halves = []
    for pos in range(0, n_picks, 2):
        k_lo, k_hi = order[pos], order[pos + 1]
        lo = tab_ref[pl.ds(ro_wins[k_lo // per_ref][k_lo % per_ref, tl], SUBLANES), :]
        hi = tab_ref[pl.ds(ro_wins[k_hi // per_ref][k_hi % per_ref, tl], SUBLANES), :]
        halves += [lo[:SLAB], hi[SLAB:]]
    return pltpu.bitcast(jnp.concatenate(halves, axis=0), BF16)


def _token_windows(ro_refs, tg):
    toks = pl.ds(pl.multiple_of(tg * TOKENS_PER_ITER, TOKENS_PER_ITER), TOKENS_PER_ITER)
    return [r.at[:, toks] for r in ro_refs]


def _peer_u_kernel(*refs, layer):
    ro_refs = refs[:RO_SPLIT]
    g_ref, h8_ref, tabs_hbm, sum8_ref, w_ref, a8_scr, tab_ref, buf, slab_scr, sem = refs[RO_SPLIT:]
    tt, n_picks = g_ref.shape
    eye = (lax.broadcasted_iota(I32, (SUBLANES, n_picks * SUBLANES), 1) % SUBLANES
           == lax.broadcasted_iota(I32, (SUBLANES, n_picks * SUBLANES), 0))

    @pl.when(pl.program_id(0) == 0)
    def _():
        _fill_table(tabs_hbm, layer, tab_ref, buf, slab_scr, sem)

    def token_group(tg, carry):
        rows = []
        ro_wins = _token_windows(ro_refs, tg)
        for tl in range(TOKENS_PER_ITER):
            t = tg * TOKENS_PER_ITER + tl
            bm = _gather_slabs(ro_wins, tab_ref, tl, n_picks)
            x8 = h8_ref[pl.ds(pl.multiple_of(t * SUBLANES, SUBLANES), SUBLANES), :]
            r = _nt_dot(x8, bm)
            rows.append(jnp.sum(jnp.where(eye, r, 0.0), axis=0, keepdims=True))
        a8_scr[pl.ds(pl.multiple_of(tg * TOKENS_PER_ITER, TOKENS_PER_ITER), TOKENS_PER_ITER), :] = (
            jnp.concatenate(rows, axis=0))
        return carry

    lax.fori_loop(0, tt // TOKENS_PER_ITER, token_group, 0)
    a8 = a8_scr[...]
    hi = a8.astype(BF16)
    lo = (a8 - hi.astype(F32)).astype(BF16)
    a = jnp.dot(hi, sum8_ref[...], preferred_element_type=F32) + jnp.dot(lo, sum8_ref[...], preferred_element_type=F32)
    gelu = 0.5 * a * (1.0 + lax.erf(a * np.float32(2.0 ** -0.5)))
    w_ref[...] = g_ref[...] * gelu


def _offset_specs(tt, n_picks):
    per_ref = n_picks // RO_SPLIT
    return [pl.BlockSpec((None, per_ref, tt), lambda i, j=j: (i, j, 0), memory_space=pltpu.SMEM)
            for j in range(RO_SPLIT)]


def _peer_u(ro, g, h8, tabs, layer, sum8):
    nt, n_picks, tt = ro.shape
    ro_parts = [ro] * RO_SPLIT
    return pl.pallas_call(
        functools.partial(_peer_u_kernel, layer=layer),
        grid=(nt,),
        in_specs=_offset_specs(tt, n_picks) + [
            pl.BlockSpec((tt, n_picks), lambda i: (i, 0)),
            pl.BlockSpec((tt * SUBLANES, LANES), lambda i: (i, 0)),
            pl.BlockSpec(memory_space=pl.ANY),
            pl.BlockSpec(sum8.shape, lambda i: (0, 0)),
        ],
        out_specs=pl.BlockSpec((tt, n_picks), lambda i: (i, 0)),
        out_shape=jax.ShapeDtypeStruct((nt * tt, n_picks), F32),
        scratch_shapes=[pltpu.VMEM((tt, n_picks * SUBLANES), F32)] + _table_scratch(*tabs.shape[1:]),
        compiler_params=pltpu.CompilerParams(dimension_semantics=("arbitrary",),
                                             vmem_limit_bytes=VMEM_TABLE_LIMIT),
        name="peer_u",
    )(*ro_parts, g, h8, tabs, sum8)


def _peer_v_kernel(*refs, layer):
    ro_refs = refs[:RO_SPLIT]
    w_ref, tabs_hbm, rep8_ref, o8_ref, whi_scr, wlo_scr, tab_ref, buf, slab_scr, sem = refs[RO_SPLIT:]
    tt, n_picks = w_ref.shape

    @pl.when(pl.program_id(0) == 0)
    def _():
        _fill_table(tabs_hbm, layer, tab_ref, buf, slab_scr, sem)

    w = w_ref[...]
    hi = w.astype(BF16)
    lo = (w - hi.astype(F32)).astype(BF16)
    whi_scr[...] = jnp.dot(hi, rep8_ref[...], preferred_element_type=F32)
    wlo_scr[...] = jnp.dot(lo, rep8_ref[...], preferred_element_type=F32)
    eye = (lax.broadcasted_iota(I32, (SUBLANES, n_picks * SUBLANES), 1) % SUBLANES
           == lax.broadcasted_iota(I32, (SUBLANES, n_picks * SUBLANES), 0))

    def token_group(tg, carry):
        ro_wins = _token_windows(ro_refs, tg)
        for tl in range(TOKENS_PER_ITER):
            t = tg * TOKENS_PER_ITER + tl
            bm = _gather_slabs(ro_wins, tab_ref, tl, n_picks)
            w_hi = jnp.where(eye, whi_scr[pl.ds(t, 1), :], 0.0).astype(BF16)
            w_lo = jnp.where(eye, wlo_scr[pl.ds(t, 1), :], 0.0).astype(BF16)
            r = jnp.dot(jnp.concatenate([w_hi, w_lo], axis=0), bm, preferred_element_type=F32)
            o8_ref[pl.ds(pl.multiple_of(t * SUBLANES, SUBLANES), SUBLANES), :] = r[0:SUBLANES] + r[SUBLANES:]
        return carry

    lax.fori_loop(0, tt // TOKENS_PER_ITER, token_group, 0)


def _peer_v(ro, w, tabs, layer, rep8):
    nt, n_picks, tt = ro.shape
    ro_parts = [ro] * RO_SPLIT
    return pl.pallas_call(
        functools.partial(_peer_v_kernel, layer=layer),
        grid=(nt,),
        in_specs=_offset_specs(tt, n_picks) + [
            pl.BlockSpec((tt, n_picks), lambda i: (i, 0)),
            pl.BlockSpec(memory_space=pl.ANY),
            pl.BlockSpec(rep8.shape, lambda i: (0, 0)),
        ],
        out_specs=pl.BlockSpec((tt * SUBLANES, LANES), lambda i: (i, 0)),
        out_shape=jax.ShapeDtypeStruct((nt * tt * SUBLANES, LANES), F32),
        scratch_shapes=[pltpu.VMEM((tt, n_picks * SUBLANES), F32), pltpu.VMEM((tt, n_picks * SUBLANES), F32)]
        + _table_scratch(*tabs.shape[1:]),
        compiler_params=pltpu.CompilerParams(dimension_semantics=("arbitrary",),
                                             vmem_limit_bytes=VMEM_TABLE_LIMIT),
        name="peer_v",
    )(*ro_parts, w, tabs, rep8)


def _final_kernel(x_ref, o8_ref, mod_ref, g_ref, y_ref):
    m = mod_ref[0]
    x = x_ref[0] + m[5:6] * _slabs_to_rows(o8_ref)
    ms = jnp.mean(x * x, axis=-1, keepdims=True)
    y_ref[0] = x * lax.rsqrt(ms + EPS) * g_ref[...]


def _final(x1, o8, mod, g, tm):
    bsz, s, d = x1.shape
    spec = pl.BlockSpec((1, tm, d), lambda b, i: (b, i, 0))
    return pl.pallas_call(
        _final_kernel,
        grid=(bsz, s // tm),
        in_specs=[spec, pl.BlockSpec((tm * SUBLANES, LANES), lambda b, i: (b * (s // tm) + i, 0)),
                  pl.BlockSpec((1, N_MOD, d), lambda b, i: (b, 0, 0)),
                  pl.BlockSpec((1, d), lambda b, i: (0, 0))],
        out_specs=spec,
        out_shape=jax.ShapeDtypeStruct((bsz, s, d), F32),
        name="final",
    )(x1, o8, mod, g.reshape(1, d))


def _layer(layer, x, ctx, mods, norm1_g, w_in, pool_w, pool_scale, rpb, w_out, norm2_g, wq, keys, u_tabs, v_tabs):
    bsz, s, d = x.shape
    d_in = w_in.shape[1]
    d_pool = pool_scale.shape[0]
    d_attn = (d_in - d_pool) // 3
    mod_x = mods[:bsz].reshape(bsz, N_MOD, d)
    mod_c = mods[bsz:bsz + 1].reshape(1, N_MOD, d)
    w_in_b = w_in.astype(BF16)
    scale = float((d_attn // NA_HEADS) ** -0.5)

    tm = min(512, s)
    p, q, k, vv = _in_proj(
        x, mod_x, norm1_g, w_in_b,
        splits=((0, d_pool, 1.0), (d_pool, d_pool + d_attn, scale),
                (d_pool + d_attn, d_pool + 2 * d_attn, 1.0), (d_pool + 2 * d_attn, d_in, 1.0)),
        dtypes=(F32, BF16, BF16, BF16), tm=tm, mod_per_batch=True, name="in_proj")
    kc, vc = _in_proj(
        ctx, mod_c, norm1_g, w_in_b[:, d_pool + d_attn:],
        splits=((0, d_attn, 1.0), (d_attn, 2 * d_attn, 1.0)),
        dtypes=(BF16, BF16), tm=ctx.shape[1], mod_per_batch=False, name="ctx_proj")

    attn = _attention(q, k, vv, kc, vc, _attn_bias(rpb))
    x1, h2, h8 = _mix(p, attn, x, mod_x, pool_w.astype(BF16), pool_scale, w_out.astype(BF16), norm2_g, tm)

    n = bsz * s
    ro, g = _retrieve(h2.reshape(n, d), wq.T.astype(BF16),
                      keys.reshape(-1, PEER_NKEYS, keys.shape[-1]).astype(BF16), tq=SUBLANES * LANES)
    n_picks = PEER_HEADS * PEER_TOPK
    rep8 = _expand_matrix(n_picks)
    w = _peer_u(ro, g, h8, u_tabs, layer, rep8.T)
    o8 = _peer_v(ro, w, v_tabs, layer, rep8)
    return x1, o8, mod_x, tm


def kernel(x, c, ctx, c_ctx, ada_w, ada_b, norm1_g, w_in, pool_w, pool_scale, na_rpb, w_out, norm2_g, peer_wq,
           peer_keys, peer_u, peer_v, final_g):
    depth = ada_w.shape[0]
    assert depth == 1, "context-stream update of deeper stacks is not implemented"
    bsz, s, d = x.shape
    rows_c = -(-(bsz + 1) // SUBLANES) * SUBLANES
    cc = jnp.concatenate([c, c_ctx[None, :], jnp.zeros((rows_c - bsz - 1, d), F32)], axis=0)
    mods = _ada(cc, ada_w[0], ada_b[0])
    x1, o8, mod_x, tm = _layer(0, x, ctx, mods, norm1_g[0], w_in[0], pool_w[0], pool_scale[0], na_rpb[0], w_out[0],
                               norm2_g[0], peer_wq[0], peer_keys[0], peer_u, peer_v)
    return _final(x1, o8, mod_x, final_g, tm)
```

```python
import functools

import jax
import jax.numpy as jnp
import numpy as np
from jax import lax
from jax.experimental import pallas as pl
from jax.experimental.pallas import tpu as pltpu

F32 = jnp.float32
BF16 = jnp.bfloat16
I32 = jnp.int32
U32 = jnp.uint32

EPS = 1e-6
GRID_W = 64
POOL_WINDOWS = (2, 4, 8, 16)
NA_HEADS = 8
NA_KH = 8
NA_KW = 16
N_MOD = 6
PEER_HEADS = 8
PEER_NKEYS = 128
PEER_TOPK = 16

LANES = 128
SUBLANES = 8
DENSE_TILE = 512
RETRIEVE_TILE = SUBLANES * LANES
PEER_TILE = LANES
ROWS_PER_QBLOCK = 4
ROWS_PER_KWINDOW = 12
NEG = -1e30
SLAB = 4
TABLE_PAD = 8
RO_SPLIT = 8
TABLE_FILL_ROWS = 256
TOKENS_PER_ITER = 64
VMEM_TABLE_LIMIT = 48 * 1024 * 1024

CAND_COUNTS = tuple(PEER_TOPK // (i + 1) for i in range(PEER_TOPK))
N_CAND = sum(CAND_COUNTS)


def _nt_dot(a, b):
    return lax.dot_general(a, b, (((1,), (1,)), ((), ())), preferred_element_type=F32)


def _ada_kernel(c_ref, w_ref, b_ref, o_ref):
    s = jax.nn.silu(c_ref[...])
    o_ref[...] = jnp.dot(s.astype(BF16), w_ref[...].astype(BF16), preferred_element_type=F32) + b_ref[...]


def _ada(cc, w, b):
    m, d = cc.shape
    n = w.shape[1]
    return pl.pallas_call(
        _ada_kernel,
        grid=(n // d,),
        in_specs=[
            pl.BlockSpec((m, d), lambda j: (0, 0)),
            pl.BlockSpec((d, d), lambda j: (0, j)),
            pl.BlockSpec((1, d), lambda j: (0, j)),
        ],
        out_specs=pl.BlockSpec((m, d), lambda j: (0, j)),
        out_shape=jax.ShapeDtypeStruct((m, n), F32),
        name="ada",
    )(cc, w, b.reshape(1, n))


def _rows_to_slabs(x, slab_ref):
    t = x.shape[0]
    for j in range(SUBLANES):
        slab_ref[pl.ds(j, t, stride=SUBLANES), :] = x[:, LANES * j:LANES * (j + 1)]


def _slabs_to_rows(slab_ref):
    t = slab_ref.shape[0] // SUBLANES
    return jnp.concatenate([slab_ref[pl.ds(j, t, stride=SUBLANES), :] for j in range(SUBLANES)], axis=-1)


def _rmsnorm(x, g):
    ms = jnp.mean(x * x, axis=-1, keepdims=True)
    return x * lax.rsqrt(ms + EPS) * g


def _norm_mod(x, g, shift, scale):
    return _rmsnorm(x, g) * (1.0 + scale) + shift


def _in_proj_kernel(x_ref, mod_ref, g_ref, w_ref, *out_refs, splits):
    m = mod_ref[0]
    h = _norm_mod(x_ref[0], g_ref[...], m[0:1], m[1:2])
    z = jnp.dot(h.astype(BF16), w_ref[...], preferred_element_type=F32)
    for o_ref, (a, b, mult) in zip(out_refs, splits):
        v = z[:, a:b]
        if mult != 1.0:
            v = v * mult
        o_ref[0] = v.astype(o_ref.dtype)


def _in_proj(x, mod, g, w, splits, dtypes, tm, mod_per_batch, name):
    bsz, s, d = x.shape
    n = w.shape[1]
    mod_map = (lambda b, i: (b, 0, 0)) if mod_per_batch else (lambda b, i: (0, 0, 0))
    return pl.pallas_call(
        functools.partial(_in_proj_kernel, splits=splits),
        grid=(bsz, s // tm),
        in_specs=[
            pl.BlockSpec((1, tm, d), lambda b, i: (b, i, 0)),
            pl.BlockSpec((1, N_MOD, d), mod_map),
            pl.BlockSpec((1, d), lambda b, i: (0, 0)),
            pl.BlockSpec((d, n), lambda b, i: (0, 0)),
        ],
        out_specs=[pl.BlockSpec((1, tm, b - a), lambda b_, i: (b_, i, 0)) for (a, b, _) in splits],
        out_shape=[jax.ShapeDtypeStruct((bsz, s, b - a), dt) for (a, b, _), dt in zip(splits, dtypes)],
        name=name,
    )(x, mod, g.reshape(1, d), w)


def _attn_bias(rpb):
    n_heads = rpb.shape[0]
    m = np.arange(2 * NA_KW - 1)[:, None, None]
    qc = np.arange(GRID_W)[None, :, None]
    kc = np.arange(GRID_W)[None, None, :]
    onehot = (kc - qc + NA_KW - 1 == m).astype(np.float32)
    c0 = np.clip(qc - NA_KW // 2, 0, GRID_W - NA_KW)
    col_ok = (kc >= c0) & (kc < c0 + NA_KW)
    cb = jnp.einsum("ham,mqk->haqk", rpb.astype(F32), onehot, precision=lax.Precision.HIGHEST)
    cb = jnp.where(col_ok, cb, NEG)
    masked = jnp.full((n_heads, GRID_W, GRID_W), NEG, F32)
    out = []
    for off, j0 in ((0, lambda qr: 0), (-4, lambda qr: qr), (-8, lambda qr: 4)):
        per_row = []
        for qr in range(ROWS_PER_QBLOCK):
            blocks = [cb[:, off + j - qr + NA_KH - 1] if j0(qr) <= j < j0(qr) + NA_KH else masked
                      for j in range(ROWS_PER_KWINDOW)]
            per_row.append(jnp.concatenate(blocks, axis=-1))
        out.append(jnp.concatenate(per_row, axis=1))
    return jnp.stack(out)


def _attn_kernel(q_ref, k0_ref, k1_ref, k2_ref, v0_ref, v1_ref, v2_ref, kc_ref, vc_ref, bias_ref, o_ref):
    tq = q_ref.shape[1]
    low = lax.broadcasted_iota(I32, (tq, LANES), 1) < (LANES // 2)
    for hp in range(NA_HEADS // 2):
        sl = slice(LANES * hp, LANES * (hp + 1))
        q = q_ref[0, :, sl]
        kw = jnp.concatenate([k0_ref[0, :, sl], k1_ref[0, :, sl], k2_ref[0, :, sl]], axis=0)
        vw = jnp.concatenate([v0_ref[0, :, sl], v1_ref[0, :, sl], v2_ref[0, :, sl]], axis=0)
        kcp = kc_ref[0, :, sl]
        vcp = vc_ref[0, :, sl]
        outs = []
        for hh in range(2):
            qm = jnp.where(low if hh == 0 else jnp.logical_not(low), q, jnp.zeros_like(q))
            s_loc = _nt_dot(qm, kw) + bias_ref[0, 2 * hp + hh]
            s_ctx = _nt_dot(qm, kcp)
            m = jnp.maximum(jnp.max(s_loc, axis=-1, keepdims=True), jnp.max(s_ctx, axis=-1, keepdims=True))
            e_loc = jnp.exp(s_loc - m)
            e_ctx = jnp.exp(s_ctx - m)
            den = jnp.sum(e_loc, axis=-1, keepdims=True) + jnp.sum(e_ctx, axis=-1, keepdims=True)
            o = jnp.dot(e_loc.astype(BF16), vw, preferred_element_type=F32)
            o = o + jnp.dot(e_ctx.astype(BF16), vcp, preferred_element_type=F32)
            outs.append(o / den)
        o_ref[0, :, sl] = jnp.where(low, outs[0], outs[1]).astype(o_ref.dtype)


def _attention(q, k, v, kc, vc, bias):
    bsz, s, da = q.shape
    c = kc.shape[1]
    tq = ROWS_PER_QBLOCK * GRID_W
    nb = s // tq
    assert nb >= 3, "needs at least 12 grid rows"

    def kv_map(jj):
        return lambda i, b: (b, jnp.clip(i - 1, 0, nb - 3) + jj, 0)

    def bias_map(i, b):
        return (jnp.where(i == 0, 0, jnp.where(i == nb - 1, 2, 1)), 0, 0, 0)

    kv_specs = [pl.BlockSpec((1, tq, da), kv_map(jj)) for jj in range(3)]
    return pl.pallas_call(
        _attn_kernel,
        grid=(nb, bsz),
        in_specs=[pl.BlockSpec((1, tq, da), lambda i, b: (b, i, 0))] + kv_specs + kv_specs + [
            pl.BlockSpec((1, c, da), lambda i, b: (b, 0, 0)),
            pl.BlockSpec((1, c, da), lambda i, b: (b, 0, 0)),
            pl.BlockSpec((1, NA_HEADS, tq, ROWS_PER_KWINDOW * GRID_W), bias_map),
        ],
        out_specs=pl.BlockSpec((1, tq, da), lambda i, b: (b, i, 0)),
        out_shape=jax.ShapeDtypeStruct((bsz, s, da), BF16),
        compiler_params=pltpu.CompilerParams(vmem_limit_bytes=VMEM_TABLE_LIMIT),
        name="attn",
    )(q, k, k, k, v, v, v, kc, vc, bias)


def _mix_kernel(p_ref, pprev_ref, pnext_ref, a_ref, x_ref, mod_ref, pw_ref, ps_ref, wo_ref, g2_ref,
                x1_ref, h2_ref, h8_ref, slab_scr, *, seq_len):
    tm = p_ref.shape[1]
    dp = p_ref.shape[2]
    i = pl.program_id(1)
    last = pl.num_programs(1) - 1
    p = p_ref[0]
    prev = jnp.where(i > 0, pprev_ref[0], 0.0)
    nxt = jnp.where(i < last, pnext_ref[0], 0.0)
    pe = jnp.concatenate([prev, p, nxt], axis=0)
    t = i * tm + lax.broadcasted_iota(I32, (tm, 1), 0)
    ys = []
    for g, w in enumerate(POOL_WINDOWS):
        hw = w // 2
        sl = slice(LANES * g, LANES * (g + 1))
        pg = pe[:, sl]
        acc = pg[SUBLANES - hw:SUBLANES - hw + tm]
        for d in range(-hw + 1, hw):
            acc = acc + pg[SUBLANES + d:SUBLANES + d + tm]
        cnt = (jnp.minimum(t + hw, seq_len) - jnp.maximum(t - hw, 0)).astype(F32)
        pooled = acc / cnt - p[:, sl]
        y = jnp.dot(pooled.astype(BF16), pw_ref[g], preferred_element_type=F32)
        ys.append((y * ps_ref[:, sl]).astype(BF16))
    pool_x = jnp.concatenate(ys, axis=-1)
    mixed = jnp.dot(pool_x, wo_ref[0:dp, :], preferred_element_type=F32)
    mixed = mixed + jnp.dot(a_ref[0], wo_ref[dp:, :], preferred_element_type=F32)
    m = mod_ref[0]
    x1 = x_ref[0] + m[2:3] * mixed
    x1_ref[0] = x1
    h2 = _norm_mod(x1, g2_ref[...], m[3:4], m[4:5])
    h2_ref[0] = h2.astype(h2_ref.dtype)
    _rows_to_slabs(h2, slab_scr)
    h8_ref[...] = slab_scr[...].astype(h8_ref.dtype)


def _mix(p, attn, x, mod, pool_w, pool_scale, w_out, norm2_g, tm):
    bsz, s, d = x.shape
    dp = p.shape[2]
    da = attn.shape[2]
    nblk8 = s // SUBLANES
    r = tm // SUBLANES
    return pl.pallas_call(
        functools.partial(_mix_kernel, seq_len=s),
        grid=(bsz, s // tm),
        in_specs=[
            pl.BlockSpec((1, tm, dp), lambda b, i: (b, i, 0)),
            pl.BlockSpec((1, SUBLANES, dp), lambda b, i: (b, jnp.maximum(i * r - 1, 0), 0)),
            pl.BlockSpec((1, SUBLANES, dp), lambda b, i: (b, jnp.minimum((i + 1) * r, nblk8 - 1), 0)),
            pl.BlockSpec((1, tm, da), lambda b, i: (b, i, 0)),
            pl.BlockSpec((1, tm, d), lambda b, i: (b, i, 0)),
            pl.BlockSpec((1, N_MOD, d), lambda b, i: (b, 0, 0)),
            pl.BlockSpec(pool_w.shape, lambda b, i: (0, 0, 0)),
            pl.BlockSpec((1, dp), lambda b, i: (0, 0)),
            pl.BlockSpec(w_out.shape, lambda b, i: (0, 0)),
            pl.BlockSpec((1, d), lambda b, i: (0, 0)),
        ],
        out_specs=[
            pl.BlockSpec((1, tm, d), lambda b, i: (b, i, 0)),
            pl.BlockSpec((1, tm, d), lambda b, i: (b, i, 0)),
            pl.BlockSpec((tm * SUBLANES, LANES), lambda b, i: (b * (s // tm) + i, 0)),
        ],
        out_shape=[jax.ShapeDtypeStruct((bsz, s, d), F32), jax.ShapeDtypeStruct((bsz, s, d), BF16),
                   jax.ShapeDtypeStruct((bsz * s * SUBLANES, LANES), BF16)],
        scratch_shapes=[pltpu.VMEM((tm * SUBLANES, LANES), F32)],
        name="mix",
    )(p, p, p, attn, x, mod, pool_w, pool_scale.reshape(1, dp), w_out, norm2_g.reshape(1, d))


def _argmax_tree(value, payload, lo, hi):
    if hi - lo == 1:
        return value(lo), payload(lo)
    mid = (lo + hi) // 2
    va, pa = _argmax_tree(value, payload, lo, mid)
    vb, pb = _argmax_tree(value, payload, mid, hi)
    return jnp.maximum(va, vb), jnp.where(vb > va, pb, pa).astype(I32)


def _tile(k):
    return pl.ds(k * SUBLANES, SUBLANES)


def _retrieve_kernel(h_ref, wqt_ref, keys_ref, ro_ref, g_ref, qt_scr, s_scr, sv_scr, si_scr, cand_scr, cidx_scr,
                     gt_scr, et_scr, tv_scr, te_scr):
    tq = h_ref.shape[0]
    n_hp = keys_ref.shape[0]
    assert tq == SUBLANES * LANES
    neg_inf = jnp.full((SUBLANES, LANES), -jnp.inf, F32)
    qt_scr[...] = _nt_dot(wqt_ref[...], h_ref[...])

    def first_level(hp, carry):
        q = qt_scr[pl.ds(pl.multiple_of(hp * PEER_NKEYS, PEER_NKEYS), PEER_NKEYS), :]
        s = jnp.dot(keys_ref[hp], q.astype(BF16), preferred_element_type=F32)
        _rows_to_slabs(s, s_scr)

        def extract(r, carry):
            m, idx = _argmax_tree(lambda k: s_scr[_tile(k), :], lambda k: k, 0, PEER_NKEYS)
            sv_scr[hp, r] = m
            si_scr[hp, r] = idx
            for k in range(PEER_NKEYS):
                s_scr[_tile(k), :] = jnp.where(idx == k, neg_inf, s_scr[_tile(k), :])
            return carry

        lax.fori_loop(0, PEER_TOPK, extract, 0)
        return carry

    lax.fori_loop(0, n_hp, first_level, 0)

    cells = [(i, j) for i, n in enumerate(CAND_COUNTS) for j in range(n)]

    def second_level(h, carry):
        for c, (i, j) in enumerate(cells):
            cand_scr[_tile(c), :] = sv_scr[2 * h, i] + sv_scr[2 * h + 1, j]
            cidx_scr[_tile(c), :] = si_scr[2 * h, i] * PEER_NKEYS + si_scr[2 * h + 1, j]

        def extract(r, carry):
            m, e = _argmax_tree(lambda c: cand_scr[_tile(c), :], lambda c: cidx_scr[_tile(c), :], 0, N_CAND)
            tv_scr[r] = m
            te_scr[r] = e
            for c in range(N_CAND):
                cand_scr[_tile(c), :] = jnp.where(cidx_scr[_tile(c), :] == e, neg_inf, cand_scr[_tile(c), :])
            return carry

        lax.fori_loop(0, PEER_TOPK, extract, 0)
        es = [jnp.exp(tv_scr[r] - tv_scr[0]) for r in range(PEER_TOPK)]
        den = es[0]
        for e in es[1:]:
            den = den + e
        for r in range(PEER_TOPK):
            row = pl.ds(pl.multiple_of((h * PEER_TOPK + r) * SUBLANES, SUBLANES), SUBLANES)
            gt_scr[row, :] = es[r] / den
            et_scr[row, :] = te_scr[r] * SLAB + (TABLE_PAD if r % 2 == 0 else TABLE_PAD - SLAB)
        return carry

    lax.fori_loop(0, n_hp // 2, second_level, 0)

    g_rows = _slabs_to_rows(gt_scr)
    ro_rows = _slabs_to_rows(et_scr)
    for c in range(tq // LANES):
        g_ref[c * LANES:(c + 1) * LANES, :] = g_rows[:, c * LANES:(c + 1) * LANES].T
        ro_ref[c] = ro_rows[:, c * LANES:(c + 1) * LANES]


def _retrieve(h2, wqt, keys, tq):
    n, d = h2.shape
    nq = wqt.shape[0]
    n_hp = keys.shape[0]
    n_picks = PEER_HEADS * PEER_TOPK
    return pl.pallas_call(
        _retrieve_kernel,
        grid=(n // tq,),
        in_specs=[
            pl.BlockSpec((tq, d), lambda i: (i, 0)),
            pl.BlockSpec((nq, d), lambda i: (0, 0)),
            pl.BlockSpec(keys.shape, lambda i: (0, 0, 0)),
        ],
        out_specs=[pl.BlockSpec((tq // LANES, n_picks, LANES), lambda i: (i, 0, 0)),
                   pl.BlockSpec((tq, n_picks), lambda i: (i, 0))],
        out_shape=[jax.ShapeDtypeStruct((n // LANES, n_picks, LANES), I32), jax.ShapeDtypeStruct((n, n_picks), F32)],
        scratch_shapes=[
            pltpu.VMEM((nq, tq), F32),
            pltpu.VMEM((PEER_NKEYS * SUBLANES, LANES), F32),
            pltpu.VMEM((n_hp, PEER_TOPK, SUBLANES, LANES), F32),
            pltpu.VMEM((n_hp, PEER_TOPK, SUBLANES, LANES), I32),
            pltpu.VMEM((N_CAND * SUBLANES, LANES), F32),
            pltpu.VMEM((N_CAND * SUBLANES, LANES), I32),
            pltpu.VMEM((n_picks * SUBLANES, LANES), F32),
            pltpu.VMEM((n_picks * SUBLANES, LANES), I32),
            pltpu.VMEM((PEER_TOPK, SUBLANES, LANES), F32),
            pltpu.VMEM((PEER_TOPK, SUBLANES, LANES), I32),
        ],
        compiler_params=pltpu.CompilerParams(vmem_limit_bytes=VMEM_TABLE_LIMIT),
        name="retrieve",
    )(h2, wqt, keys)


def _table_scratch(n_experts, d):
    assert d == SUBLANES * LANES and n_experts % TABLE_FILL_ROWS == 0
    return [
        pltpu.VMEM((n_experts * SLAB + 2 * TABLE_PAD, LANES), U32),
        pltpu.VMEM((2, TABLE_FILL_ROWS, d), F32),
        pltpu.VMEM((TABLE_FILL_ROWS * SUBLANES, LANES), F32),
        pltpu.SemaphoreType.DMA((2,)),
    ]


def _fill_table(tabs_hbm, layer, tab_scr, buf, slab_scr, sem):
    n_experts = tabs_hbm.shape[1]
    te = buf.shape[1]
    n_chunks = n_experts // te
    pad = jnp.zeros((TABLE_PAD, LANES), U32)
    tab_scr[0:TABLE_PAD, :] = pad
    tab_scr[TABLE_PAD + n_experts * SLAB:, :] = pad

    def chunk_copy(c, slot):
        return pltpu.make_async_copy(tabs_hbm.at[layer, pl.ds(c * te, te)], buf.at[slot], sem.at[slot])

    chunk_copy(0, 0).start()

    def chunk(c, carry):
        slot = c % 2

        @pl.when(c + 1 < n_chunks)
        def _():
            chunk_copy(c + 1, 1 - slot).start()

        chunk_copy(c, slot).wait()
        _rows_to_slabs(buf[slot], slab_scr)
        rows = pl.ds(pl.multiple_of(TABLE_PAD + c * (te * SLAB), SUBLANES), te * SLAB)
        tab_scr[rows, :] = pltpu.bitcast(slab_scr[...].astype(BF16), U32)
        return carry

    lax.fori_loop(0, n_chunks, chunk, 0)


def _gather_order(n_picks):
    per_ref = n_picks // RO_SPLIT
    return [j * per_ref + m + b for m in range(0, per_ref, 2) for j in range(RO_SPLIT) for b in range(2)]


def _expand_matrix(n_picks):
    rep = np.zeros((n_picks, n_picks * SUBLANES), np.float32)
    for pos, k in enumerate(_gather_order(n_picks)):
        rep[k, pos * SUBLANES:(pos + 1) * SUBLANES] = 1.0
    return jnp.asarray(rep, BF16)


def _gather_slabs(ro_wins, tab_ref, tl, n_picks):
    per_ref = n_picks // RO_SPLIT
    order = _gather_order(n_picks)
    halves = []
    for pos in range(0, n_picks, 2):
        k_lo, k_hi = order[pos], order[pos + 1]
        lo = tab_ref[pl.ds(ro_wins[k_lo // per_ref][k_lo % per_ref, tl], SUBLANES), :]
        hi = tab_ref[pl.ds(ro_wins[k_hi // per_ref][k_hi % per_ref, tl], SUBLANES), :]
        halves += [lo[:SLAB], hi[SLAB:]]
    return pltpu.bitcast(jnp.concatenate(halves, axis=0), BF16)


def _token_windows(ro_refs, tg):
    toks = pl.ds(pl.multiple_of(tg * TOKENS_PER_ITER, TOKENS_PER_ITER), TOKENS_PER_ITER)
    return [r.at[:, toks] for r in ro_refs]


def _offset_specs(tt, n_picks):
    per_ref = n_picks // RO_SPLIT
    return [pl.BlockSpec((None, per_ref, tt), lambda i, j=j: (i, j, 0), memory_space=pltpu.SMEM)
            for j in range(RO_SPLIT)]


def _diag_mask(n_picks):
    return (lax.broadcasted_iota(I32, (SUBLANES, n_picks * SUBLANES), 1) % SUBLANES
            == lax.broadcasted_iota(I32, (SUBLANES, n_picks * SUBLANES), 0))


def _peer_u_kernel(*refs, layer):
    ro_refs = refs[:RO_SPLIT]
    g_ref, h8_ref, tabs_hbm, sum8_ref, w_ref, a8_scr, tab_ref, buf, slab_scr, sem = refs[RO_SPLIT:]
    tt, n_picks = g_ref.shape
    eye = _diag_mask(n_picks)

    @pl.when(pl.program_id(0) == 0)
    def _():
        _fill_table(tabs_hbm, layer, tab_ref, buf, slab_scr, sem)

    def token_group(tg, carry):
        rows = []
        ro_wins = _token_windows(ro_refs, tg)
        for tl in range(TOKENS_PER_ITER):
            t = tg * TOKENS_PER_ITER + tl
            bm = _gather_slabs(ro_wins, tab_ref, tl, n_picks)
            x8 = h8_ref[pl.ds(pl.multiple_of(t * SUBLANES, SUBLANES), SUBLANES), :]
            r = _nt_dot(x8, bm)
            rows.append(jnp.sum(jnp.where(eye, r, 0.0), axis=0, keepdims=True))
        a8_scr[pl.ds(pl.multiple_of(tg * TOKENS_PER_ITER, TOKENS_PER_ITER), TOKENS_PER_ITER), :] = (
            jnp.concatenate(rows, axis=0))
        return carry

    lax.fori_loop(0, tt // TOKENS_PER_ITER, token_group, 0)
    a8 = a8_scr[...]
    hi = a8.astype(BF16)
    lo = (a8 - hi.astype(F32)).astype(BF16)
    a = jnp.dot(hi, sum8_ref[...], preferred_element_type=F32) + jnp.dot(lo, sum8_ref[...], preferred_element_type=F32)
    gelu = 0.5 * a * (1.0 + lax.erf(a * np.float32(2.0 ** -0.5)))
    w_ref[...] = g_ref[...] * gelu


def _peer_u(ro, g, h8, tabs, layer, sum8):
    nt, n_picks, tt = ro.shape
    ro_parts = [ro] * RO_SPLIT
    return pl.pallas_call(
        functools.partial(_peer_u_kernel, layer=layer),
        grid=(nt,),
        in_specs=_offset_specs(tt, n_picks) + [
            pl.BlockSpec((tt, n_picks), lambda i: (i, 0)),
            pl.BlockSpec((tt * SUBLANES, LANES), lambda i: (i, 0)),
            pl.BlockSpec(memory_space=pl.ANY),
            pl.BlockSpec(sum8.shape, lambda i: (0, 0)),
        ],
        out_specs=pl.BlockSpec((tt, n_picks), lambda i: (i, 0)),
        out_shape=jax.ShapeDtypeStruct((nt * tt, n_picks), F32),
        scratch_shapes=[pltpu.VMEM((tt, n_picks * SUBLANES), F32)] + _table_scratch(*tabs.shape[1:]),
        compiler_params=pltpu.CompilerParams(dimension_semantics=("arbitrary",),
                                             vmem_limit_bytes=VMEM_TABLE_LIMIT),
        name="peer_u",
    )(*ro_parts, g, h8, tabs, sum8)


def _peer_v_kernel(*refs, layer):
    ro_refs = refs[:RO_SPLIT]
    (w_ref, tabs_hbm, rep8_ref, x1_ref, mod_ref, gf_ref, y_ref,
     whi_scr, wlo_scr, o8_scr, tab_ref, buf, slab_scr, sem) = refs[RO_SPLIT:]
    tt, n_picks = w_ref.shape

    @pl.when(pl.program_id(0) == 0)
    def _():
        _fill_table(tabs_hbm, layer, tab_ref, buf, slab_scr, sem)

    w = w_ref[...]
    hi = w.astype(BF16)
    lo = (w - hi.astype(F32)).astype(BF16)
    whi_scr[...] = jnp.dot(hi, rep8_ref[...], preferred_element_type=F32)
    wlo_scr[...] = jnp.dot(lo, rep8_ref[...], preferred_element_type=F32)
    eye = _diag_mask(n_picks)

    def token_group(tg, carry):
        ro_wins = _token_windows(ro_refs, tg)
        for tl in range(TOKENS_PER_ITER):
            t = tg * TOKENS_PER_ITER + tl
            bm = _gather_slabs(ro_wins, tab_ref, tl, n_picks)
            w_hi = jnp.where(eye, whi_scr[pl.ds(t, 1), :], 0.0).astype(BF16)
            w_lo = jnp.where(eye, wlo_scr[pl.ds(t, 1), :], 0.0).astype(BF16)
            r = jnp.dot(jnp.concatenate([w_hi, w_lo], axis=0), bm, preferred_element_type=F32)
            o8_scr[pl.ds(pl.multiple_of(t * SUBLANES, SUBLANES), SUBLANES), :] = r[0:SUBLANES] + r[SUBLANES:]
        return carry

    lax.fori_loop(0, tt // TOKENS_PER_ITER, token_group, 0)
    m = mod_ref[0]
    y_ref[...] = _rmsnorm(x1_ref[...] + m[5:6] * _slabs_to_rows(o8_scr), gf_ref[...])


def _peer_v(ro, w, tabs, layer, rep8, x1, mod, final_g):
    nt, n_picks, tt = ro.shape
    bsz, s, d = x1.shape
    tiles_per_batch = s // tt
    ro_parts = [ro] * RO_SPLIT
    return pl.pallas_call(
        functools.partial(_peer_v_kernel, layer=layer),
        grid=(nt,),
        in_specs=_offset_specs(tt, n_picks) + [
            pl.BlockSpec((tt, n_picks), lambda i: (i, 0)),
            pl.BlockSpec(memory_space=pl.ANY),
            pl.BlockSpec(rep8.shape, lambda i: (0, 0)),
            pl.BlockSpec((tt, d), lambda i: (i, 0)),
            pl.BlockSpec((1, N_MOD, d), lambda i: (i // tiles_per_batch, 0, 0)),
            pl.BlockSpec((1, d), lambda i: (0, 0)),
        ],
        out_specs=pl.BlockSpec((tt, d), lambda i: (i, 0)),
        out_shape=jax.ShapeDtypeStruct((nt * tt, d), F32),
        scratch_shapes=[pltpu.VMEM((tt, n_picks * SUBLANES), F32), pltpu.VMEM((tt, n_picks * SUBLANES), F32),
                        pltpu.VMEM((tt * SUBLANES, LANES), F32)] + _table_scratch(*tabs.shape[1:]),
        compiler_params=pltpu.CompilerParams(dimension_semantics=("arbitrary",),
                                             vmem_limit_bytes=VMEM_TABLE_LIMIT),
        name="peer_v",
    )(*ro_parts, w, tabs, rep8, x1.reshape(bsz * s, d), mod, final_g.reshape(1, d))


def _layer(layer, x, ctx, mods, norm1_g, w_in, pool_w, pool_scale, rpb, w_out, norm2_g, wq, keys, u_tabs, v_tabs,
           final_g):
    bsz, s, d = x.shape
    d_in = w_in.shape[1]
    d_pool = pool_scale.shape[0]
    d_attn = (d_in - d_pool) // 3
    mod_x = mods[:bsz].reshape(bsz, N_MOD, d)
    mod_c = mods[bsz:bsz + 1].reshape(1, N_MOD, d)
    w_in_b = w_in.astype(BF16)
    scale = float((d_attn // NA_HEADS) ** -0.5)

    tm = min(DENSE_TILE, s)
    p, q, k, vv = _in_proj(
        x, mod_x, norm1_g, w_in_b,
        splits=((0, d_pool, 1.0), (d_pool, d_pool + d_attn, scale),
                (d_pool + d_attn, d_pool + 2 * d_attn, 1.0), (d_pool + 2 * d_attn, d_in, 1.0)),
        dtypes=(F32, BF16, BF16, BF16), tm=tm, mod_per_batch=True, name="in_proj")
    kc, vc = _in_proj(
        ctx, mod_c, norm1_g, w_in_b[:, d_pool + d_attn:],
        splits=((0, d_attn, 1.0), (d_attn, 2 * d_attn, 1.0)),
        dtypes=(BF16, BF16), tm=ctx.shape[1], mod_per_batch=False, name="ctx_proj")

    attn = _attention(q, k, vv, kc, vc, _attn_bias(rpb))
    x1, h2, h8 = _mix(p, attn, x, mod_x, pool_w.astype(BF16), pool_scale, w_out.astype(BF16), norm2_g, tm)

    n = bsz * s
    ro, g = _retrieve(h2.reshape(n, d), wq.T.astype(BF16),
                      keys.reshape(-1, PEER_NKEYS, keys.shape[-1]).astype(BF16), tq=RETRIEVE_TILE)
    assert ro.shape[2] == PEER_TILE
    n_picks = PEER_HEADS * PEER_TOPK
    rep8 = _expand_matrix(n_picks)
    w = _peer_u(ro, g, h8, u_tabs, layer, rep8.T)
    y = _peer_v(ro, w, v_tabs, layer, rep8, x1, mod_x, final_g)
    return y.reshape(bsz, s, d)


def kernel(x, c, ctx, c_ctx, ada_w, ada_b, norm1_g, w_in, pool_w, pool_scale, na_rpb, w_out, norm2_g, peer_wq,
           peer_keys, peer_u, peer_v, final_g):
    depth = ada_w.shape[0]
    assert depth == 1, "context-stream update of deeper stacks is not implemented"
    bsz, s, d = x.shape
    rows_c = -(-(bsz + 1) // SUBLANES) * SUBLANES
    cc = jnp.concatenate([c, c_ctx[None, :], jnp.zeros((rows_c - bsz - 1, d), F32)], axis=0)
    mods = _ada(cc, ada_w[0], ada_b[0])
    return _layer(0, x, ctx, mods, norm1_g[0], w_in[0], pool_w[0], pool_scale[0], na_rpb[0], w_out[0],
                  norm2_g[0], peer_wq[0], peer_keys[0], peer_u, peer_v, final_g)
```

```python
import functools

import jax
import jax.numpy as jnp
import numpy as np
from jax import lax
from jax.experimental import pallas as pl
from jax.experimental.pallas import tpu as pltpu

F32 = jnp.float32
BF16 = jnp.bfloat16
I32 = jnp.int32
U32 = jnp.uint32

EPS = 1e-6
GRID_W = 64
POOL_WINDOWS = (2, 4, 8, 16)
NA_HEADS = 8
NA_KH = 8
NA_KW = 16
N_MOD = 6
PEER_HEADS = 8
PEER_NKEYS = 128
PEER_TOPK = 16

LANES = 128
SUBLANES = 8
DENSE_TILE = 512
RETRIEVE_TILE = SUBLANES * LANES
PEER_TILE = LANES
ROWS_PER_QBLOCK = 4
ROWS_PER_KWINDOW = 12
NEG = -1e30
SLAB = 4
TABLE_PAD = 8
RO_SPLIT = 8
TABLE_FILL_ROWS = 256
TOKENS_PER_ITER = 128
VMEM_TABLE_LIMIT = 48 * 1024 * 1024

CAND_COUNTS = tuple(PEER_TOPK // (i + 1) for i in range(PEER_TOPK))
N_CAND = sum(CAND_COUNTS)


def _nt_dot(a, b):
    return lax.dot_general(a, b, (((1,), (1,)), ((), ())), preferred_element_type=F32)


def _ada_kernel(c_ref, w_ref, b_ref, o_ref):
    s = jax.nn.silu(c_ref[...])
    o_ref[...] = jnp.dot(s.astype(BF16), w_ref[...].astype(BF16), preferred_element_type=F32) + b_ref[...]


def _ada(cc, w, b):
    m, d = cc.shape
    n = w.shape[1]
    return pl.pallas_call(
        _ada_kernel,
        grid=(n // d,),
        in_specs=[
            pl.BlockSpec((m, d), lambda j: (0, 0)),
            pl.BlockSpec((d, d), lambda j: (0, j)),
            pl.BlockSpec((1, d), lambda j: (0, j)),
        ],
        out_specs=pl.BlockSpec((m, d), lambda j: (0, j)),
        out_shape=jax.ShapeDtypeStruct((m, n), F32),
        name="ada",
    )(cc, w, b.reshape(1, n))


def _rows_to_slabs(x, slab_ref):
    t = x.shape[0]
    for j in range(SUBLANES):
        slab_ref[pl.ds(j, t, stride=SUBLANES), :] = x[:, LANES * j:LANES * (j + 1)]


def _slabs_to_rows(slab_ref):
    t = slab_ref.shape[0] // SUBLANES
    return jnp.concatenate([slab_ref[pl.ds(j, t, stride=SUBLANES), :] for j in range(SUBLANES)], axis=-1)


def _rmsnorm(x, g):
    ms = jnp.mean(x * x, axis=-1, keepdims=True)
    return x * lax.rsqrt(ms + EPS) * g


def _norm_mod(x, g, shift, scale):
    return _rmsnorm(x, g) * (1.0 + scale) + shift


def _in_proj_kernel(x_ref, mod_ref, g_ref, w_ref, *out_refs, splits):
    m = mod_ref[0]
    h = _norm_mod(x_ref[0], g_ref[...], m[0:1], m[1:2])
    z = jnp.dot(h.astype(BF16), w_ref[...], preferred_element_type=F32)
    for o_ref, (a, b, mult) in zip(out_refs, splits):
        v = z[:, a:b]
        if mult != 1.0:
            v = v * mult
        o_ref[0] = v.astype(o_ref.dtype)


def _in_proj(x, mod, g, w, splits, dtypes, tm, mod_per_batch, name):
    bsz, s, d = x.shape
    n = w.shape[1]
    mod_map = (lambda b, i: (b, 0, 0)) if mod_per_batch else (lambda b, i: (0, 0, 0))
    return pl.pallas_call(
        functools.partial(_in_proj_kernel, splits=splits),
        grid=(bsz, s // tm),
        in_specs=[
            pl.BlockSpec((1, tm, d), lambda b, i: (b, i, 0)),
            pl.BlockSpec((1, N_MOD, d), mod_map),
            pl.BlockSpec((1, d), lambda b, i: (0, 0)),
            pl.BlockSpec((d, n), lambda b, i: (0, 0)),
        ],
        out_specs=[pl.BlockSpec((1, tm, b - a), lambda b_, i: (b_, i, 0)) for (a, b, _) in splits],
        out_shape=[jax.ShapeDtypeStruct((bsz, s, b - a), dt) for (a, b, _), dt in zip(splits, dtypes)],
        name=name,
    )(x, mod, g.reshape(1, d), w)


def _attn_bias(rpb):
    n_heads = rpb.shape[0]
    m = np.arange(2 * NA_KW - 1)[:, None, None]
    qc = np.arange(GRID_W)[None, :, None]
    kc = np.arange(GRID_W)[None, None, :]
    onehot = (kc - qc + NA_KW - 1 == m).astype(np.float32)
    c0 = np.clip(qc - NA_KW // 2, 0, GRID_W - NA_KW)
    col_ok = (kc >= c0) & (kc < c0 + NA_KW)
    cb = jnp.einsum("ham,mqk->haqk", rpb.astype(F32), onehot, precision=lax.Precision.HIGHEST)
    cb = jnp.where(col_ok, cb, NEG)
    masked = jnp.full((n_heads, GRID_W, GRID_W), NEG, F32)
    out = []
    for off, j0 in ((0, lambda qr: 0), (-4, lambda qr: qr), (-8, lambda qr: 4)):
        per_row = []
        for qr in range(ROWS_PER_QBLOCK):
            blocks = [cb[:, off + j - qr + NA_KH - 1] if j0(qr) <= j < j0(qr) + NA_KH else masked
                      for j in range(ROWS_PER_KWINDOW)]
            per_row.append(jnp.concatenate(blocks, axis=-1))
        out.append(jnp.concatenate(per_row, axis=1))
    return jnp.stack(out)


def _attn_kernel(q_ref, k0_ref, k1_ref, k2_ref, v0_ref, v1_ref, v2_ref, kc_ref, vc_ref, bias_ref, o_ref):
    tq = q_ref.shape[1]
    low = lax.broadcasted_iota(I32, (tq, LANES), 1) < (LANES // 2)
    for hp in range(NA_HEADS // 2):
        sl = slice(LANES * hp, LANES * (hp + 1))
        q = q_ref[0, :, sl]
        kw = jnp.concatenate([k0_ref[0, :, sl], k1_ref[0, :, sl], k2_ref[0, :, sl]], axis=0)
        vw = jnp.concatenate([v0_ref[0, :, sl], v1_ref[0, :, sl], v2_ref[0, :, sl]], axis=0)
        kcp = kc_ref[0, :, sl]
        vcp = vc_ref[0, :, sl]
        outs = []
        for hh in range(2):
            qm = jnp.where(low if hh == 0 else jnp.logical_not(low), q, jnp.zeros_like(q))
            s_loc = _nt_dot(qm, kw) + bias_ref[0, 2 * hp + hh]
            s_ctx = _nt_dot(qm, kcp)
            m = jnp.maximum(jnp.max(s_loc, axis=-1, keepdims=True), jnp.max(s_ctx, axis=-1, keepdims=True))
            e_loc = jnp.exp(s_loc - m)
            e_ctx = jnp.exp(s_ctx - m)
            den = jnp.sum(e_loc, axis=-1, keepdims=True) + jnp.sum(e_ctx, axis=-1, keepdims=True)
            o = jnp.dot(e_loc.astype(BF16), vw, preferred_element_type=F32)
            o = o + jnp.dot(e_ctx.astype(BF16), vcp, preferred_element_type=F32)
            outs.append(o / den)
        o_ref[0, :, sl] = jnp.where(low, outs[0], outs[1]).astype(o_ref.dtype)


def _attention(q, k, v, kc, vc, bias):
    bsz, s, da = q.shape
    c = kc.shape[1]
    tq = ROWS_PER_QBLOCK * GRID_W
    nb = s // tq
    assert nb >= 3, "needs at least 12 grid rows"

    def kv_map(jj):
        return lambda i, b: (b, jnp.clip(i - 1, 0, nb - 3) + jj, 0)

    def bias_map(i, b):
        return (jnp.where(i == 0, 0, jnp.where(i == nb - 1, 2, 1)), 0, 0, 0)

    kv_specs = [pl.BlockSpec((1, tq, da), kv_map(jj)) for jj in range(3)]
    return pl.pallas_call(
        _attn_kernel,
        grid=(nb, bsz),
        in_specs=[pl.BlockSpec((1, tq, da), lambda i, b: (b, i, 0))] + kv_specs + kv_specs + [
            pl.BlockSpec((1, c, da), lambda i, b: (b, 0, 0)),
            pl.BlockSpec((1, c, da), lambda i, b: (b, 0, 0)),
            pl.BlockSpec((1, NA_HEADS, tq, ROWS_PER_KWINDOW * GRID_W), bias_map),
        ],
        out_specs=pl.BlockSpec((1, tq, da), lambda i, b: (b, i, 0)),
        out_shape=jax.ShapeDtypeStruct((bsz, s, da), BF16),
        compiler_params=pltpu.CompilerParams(vmem_limit_bytes=VMEM_TABLE_LIMIT),
        name="attn",
    )(q, k, k, k, v, v, v, kc, vc, bias)


def _mix_kernel(p_ref, pprev_ref, pnext_ref, a_ref, x_ref, mod_ref, pw_ref, ps_ref, wo_ref, g2_ref,
                x1_ref, h2_ref, h8_ref, slab_scr, *, seq_len):
    tm = p_ref.shape[1]
    dp = p_ref.shape[2]
    i = pl.program_id(1)
    last = pl.num_programs(1) - 1
    p = p_ref[0]
    prev = jnp.where(i > 0, pprev_ref[0], 0.0)
    nxt = jnp.where(i < last, pnext_ref[0], 0.0)
    pe = jnp.concatenate([prev, p, nxt], axis=0)
    t = i * tm + lax.broadcasted_iota(I32, (tm, 1), 0)
    ys = []
    for g, w in enumerate(POOL_WINDOWS):
        hw = w // 2
        sl = slice(LANES * g, LANES * (g + 1))
        fwd = pe[:, sl]
        cw = 1
        while cw < w:
            fwd = fwd[:fwd.shape[0] - cw] + fwd[cw:]
            cw *= 2
        acc = fwd[SUBLANES - hw:SUBLANES - hw + tm]
        cnt = (jnp.minimum(t + hw, seq_len) - jnp.maximum(t - hw, 0)).astype(F32)
        pooled = acc / cnt - p[:, sl]
        y = jnp.dot(pooled.astype(BF16), pw_ref[g], preferred_element_type=F32)
        ys.append((y * ps_ref[:, sl]).astype(BF16))
    pool_x = jnp.concatenate(ys, axis=-1)
    mixed = jnp.dot(pool_x, wo_ref[0:dp, :], preferred_element_type=F32)
    mixed = mixed + jnp.dot(a_ref[0], wo_ref[dp:, :], preferred_element_type=F32)
    m = mod_ref[0]
    x1 = x_ref[0] + m[2:3] * mixed
    x1_ref[0] = x1
    h2 = _norm_mod(x1, g2_ref[...], m[3:4], m[4:5])
    h2_ref[0] = h2.astype(h2_ref.dtype)
    _rows_to_slabs(h2, slab_scr)
    h8_ref[...] = slab_scr[...].astype(h8_ref.dtype)


def _mix(p, attn, x, mod, pool_w, pool_scale, w_out, norm2_g, tm):
    bsz, s, d = x.shape
    dp = p.shape[2]
    da = attn.shape[2]
    nblk8 = s // SUBLANES
    r = tm // SUBLANES
    return pl.pallas_call(
        functools.partial(_mix_kernel, seq_len=s),
        grid=(bsz, s // tm),
        in_specs=[
            pl.BlockSpec((1, tm, dp), lambda b, i: (b, i, 0)),
            pl.BlockSpec((1, SUBLANES, dp), lambda b, i: (b, jnp.maximum(i * r - 1, 0), 0)),
            pl.BlockSpec((1, SUBLANES, dp), lambda b, i: (b, jnp.minimum((i + 1) * r, nblk8 - 1), 0)),
            pl.BlockSpec((1, tm, da), lambda b, i: (b, i, 0)),
            pl.BlockSpec((1, tm, d), lambda b, i: (b, i, 0)),
            pl.BlockSpec((1, N_MOD, d), lambda b, i: (b, 0, 0)),
            pl.BlockSpec(pool_w.shape, lambda b, i: (0, 0, 0)),
            pl.BlockSpec((1, dp), lambda b, i: (0, 0)),
            pl.BlockSpec(w_out.shape, lambda b, i: (0, 0)),
            pl.BlockSpec((1, d), lambda b, i: (0, 0)),
        ],
        out_specs=[
            pl.BlockSpec((1, tm, d), lambda b, i: (b, i, 0)),
            pl.BlockSpec((1, tm, d), lambda b, i: (b, i, 0)),
            pl.BlockSpec((tm * SUBLANES, LANES), lambda b, i: (b * (s // tm) + i, 0)),
        ],
        out_shape=[jax.ShapeDtypeStruct((bsz, s, d), F32), jax.ShapeDtypeStruct((bsz, s, d), BF16),
                   jax.ShapeDtypeStruct((bsz * s * SUBLANES, LANES), BF16)],
        scratch_shapes=[pltpu.VMEM((tm * SUBLANES, LANES), F32)],
        name="mix",
    )(p, p, p, attn, x, mod, pool_w, pool_scale.reshape(1, dp), w_out, norm2_g.reshape(1, d))


def _argmax_tree(value, payload, lo, hi):
    if hi - lo == 1:
        return value(lo), payload(lo)
    mid = (lo + hi) // 2
    va, pa = _argmax_tree(value, payload, lo, mid)
    vb, pb = _argmax_tree(value, payload, mid, hi)
    return jnp.maximum(va, vb), jnp.where(vb > va, pb, pa).astype(I32)


def _tile(k):
    return pl.ds(k * SUBLANES, SUBLANES)


def _retrieve_kernel(h_ref, wqt_ref, keys_ref, ro_ref, g_ref, qt_scr, s_scr, sv_scr, si_scr, cand_scr, cidx_scr,
                     gt_scr, et_scr, tv_scr, te_scr):
    tq = h_ref.shape[0]
    n_hp = keys_ref.shape[0]
    assert tq == SUBLANES * LANES
    neg_inf = jnp.full((SUBLANES, LANES), -jnp.inf, F32)
    qt_scr[...] = _nt_dot(wqt_ref[...], h_ref[...])

    def first_level(hp, carry):
        q = qt_scr[pl.ds(pl.multiple_of(hp * PEER_NKEYS, PEER_NKEYS), PEER_NKEYS), :]
        s = jnp.dot(keys_ref[hp], q.astype(BF16), preferred_element_type=F32)
        _rows_to_slabs(s, s_scr)

        def extract(r, carry):
            m, idx = _argmax_tree(lambda k: s_scr[_tile(k), :], lambda k: k, 0, PEER_NKEYS)
            sv_scr[hp, r] = m
            si_scr[hp, r] = idx
            for k in range(PEER_NKEYS):
                s_scr[_tile(k), :] = jnp.where(idx == k, neg_inf, s_scr[_tile(k), :])
            return carry

        lax.fori_loop(0, PEER_TOPK, extract, 0)
        return carry

    lax.fori_loop(0, n_hp, first_level, 0)

    cells = [(i, j) for i, n in enumerate(CAND_COUNTS) for j in range(n)]

    def second_level(h, carry):
        for c, (i, j) in enumerate(cells):
            cand_scr[_tile(c), :] = sv_scr[2 * h, i] + sv_scr[2 * h + 1, j]
            cidx_scr[_tile(c), :] = si_scr[2 * h, i] * PEER_NKEYS + si_scr[2 * h + 1, j]

        def extract(r, carry):
            m, e = _argmax_tree(lambda c: cand_scr[_tile(c), :], lambda c: cidx_scr[_tile(c), :], 0, N_CAND)
            tv_scr[r] = m
            te_scr[r] = e
            for c in range(N_CAND):
                cand_scr[_tile(c), :] = jnp.where(cidx_scr[_tile(c), :] == e, neg_inf, cand_scr[_tile(c), :])
            return carry

        lax.fori_loop(0, PEER_TOPK, extract, 0)
        es = [jnp.exp(tv_scr[r] - tv_scr[0]) for r in range(PEER_TOPK)]
        den = es[0]
        for e in es[1:]:
            den = den + e
        for r in range(PEER_TOPK):
            row = pl.ds(pl.multiple_of((h * PEER_TOPK + r) * SUBLANES, SUBLANES), SUBLANES)
            gt_scr[row, :] = es[r] / den
            et_scr[row, :] = te_scr[r] * SLAB + (TABLE_PAD if r % 2 == 0 else TABLE_PAD - SLAB)
        return carry

    lax.fori_loop(0, n_hp // 2, second_level, 0)

    g_rows = _slabs_to_rows(gt_scr)
    ro_rows = _slabs_to_rows(et_scr)
    for c in range(tq // LANES):
        g_ref[c * LANES:(c + 1) * LANES, :] = g_rows[:, c * LANES:(c + 1) * LANES].T
        ro_ref[c] = ro_rows[:, c * LANES:(c + 1) * LANES]


def _retrieve(h2, wqt, keys, tq):
    n, d = h2.shape
    nq = wqt.shape[0]
    n_hp = keys.shape[0]
    n_picks = PEER_HEADS * PEER_TOPK
    return pl.pallas_call(
        _retrieve_kernel,
        grid=(n // tq,),
        in_specs=[
            pl.BlockSpec((tq, d), lambda i: (i, 0)),
            pl.BlockSpec((nq, d), lambda i: (0, 0)),
            pl.BlockSpec(keys.shape, lambda i: (0, 0, 0)),
        ],
        out_specs=[pl.BlockSpec((tq // LANES, n_picks, LANES), lambda i: (i, 0, 0)),
                   pl.BlockSpec((tq, n_picks), lambda i: (i, 0))],
        out_shape=[jax.ShapeDtypeStruct((n // LANES, n_picks, LANES), I32), jax.ShapeDtypeStruct((n, n_picks), F32)],
        scratch_shapes=[
            pltpu.VMEM((nq, tq), F32),
            pltpu.VMEM((PEER_NKEYS * SUBLANES, LANES), F32),
            pltpu.VMEM((n_hp, PEER_TOPK, SUBLANES, LANES), F32),
            pltpu.VMEM((n_hp, PEER_TOPK, SUBLANES, LANES), I32),
            pltpu.VMEM((N_CAND * SUBLANES, LANES), F32),
            pltpu.VMEM((N_CAND * SUBLANES, LANES), I32),
            pltpu.VMEM((n_picks * SUBLANES, LANES), F32),
            pltpu.VMEM((n_picks * SUBLANES, LANES), I32),
            pltpu.VMEM((PEER_TOPK, SUBLANES, LANES), F32),
            pltpu.VMEM((PEER_TOPK, SUBLANES, LANES), I32),
        ],
        compiler_params=pltpu.CompilerParams(vmem_limit_bytes=VMEM_TABLE_LIMIT),
        name="retrieve",
    )(h2, wqt, keys)


def _table_scratch(n_experts, d):
    assert d == SUBLANES * LANES and n_experts % TABLE_FILL_ROWS == 0
    return [
        pltpu.VMEM((n_experts * SLAB + 2 * TABLE_PAD, LANES), U32),
        pltpu.VMEM((2, TABLE_FILL_ROWS, d), F32),
        pltpu.VMEM((TABLE_FILL_ROWS * SUBLANES, LANES), F32),
        pltpu.SemaphoreType.DMA((2,)),
    ]


def _fill_table(tabs_hbm, layer, tab_scr, buf, slab_scr, sem):
    n_experts = tabs_hbm.shape[1]
    te = buf.shape[1]
    n_chunks = n_experts // te
    pad = jnp.zeros((TABLE_PAD, LANES), U32)
    tab_scr[0:TABLE_PAD, :] = pad
    tab_scr[TABLE_PAD + n_experts * SLAB:, :] = pad

    def chunk_copy(c, slot):
        return pltpu.make_async_copy(tabs_hbm.at[layer, pl.ds(c * te, te)], buf.at[slot], sem.at[slot])

    chunk_copy(0, 0).start()

    def chunk(c, carry):
        slot = c % 2

        @pl.when(c + 1 < n_chunks)
        def _():
            chunk_copy(c + 1, 1 - slot).start()

        chunk_copy(c, slot).wait()
        _rows_to_slabs(buf[slot], slab_scr)
        rows = pl.ds(pl.multiple_of(TABLE_PAD + c * (te * SLAB), SUBLANES), te * SLAB)
        tab_scr[rows, :] = pltpu.bitcast(slab_scr[...].astype(BF16), U32)
        return carry

    lax.fori_loop(0, n_chunks, chunk, 0)


def _gather_order(n_picks):
    per_ref = n_picks // RO_SPLIT
    return [j * per_ref + m + b for m in range(0, per_ref, 2) for j in range(RO_SPLIT) for b in range(2)]


def _expand_matrix(n_picks):
    rep = np.zeros((n_picks, n_picks * SUBLANES), np.float32)
    for pos, k in enumerate(_gather_order(n_picks)):
        rep[k, pos * SUBLANES:(pos + 1) * SUBLANES] = 1.0
    return jnp.asarray(rep, BF16)


def _gather_slabs(ro_wins, tab_ref, tl, n_picks):
    per_ref = n_picks // RO_SPLIT
    order = _gather_order(n_picks)
    halves = []
    for pos in range(0, n_picks, 2):
        k_lo, k_hi = order[pos], order[pos + 1]
        lo = tab_ref[pl.ds(ro_wins[k_lo // per_ref][k_lo % per_ref, tl], SUBLANES), :]
        hi = tab_ref[pl.ds(ro_wins[k_hi // per_ref][k_hi % per_ref, tl], SUBLANES), :]
        halves += [lo[:SLAB], hi[SLAB:]]
    return pltpu.bitcast(jnp.concatenate(halves, axis=0), BF16)


def _token_windows(ro_refs, tg):
    toks = pl.ds(pl.multiple_of(tg * TOKENS_PER_ITER, TOKENS_PER_ITER), TOKENS_PER_ITER)
    return [r.at[:, toks] for r in ro_refs]


def _offset_specs(tt, n_picks):
    per_ref = n_picks // RO_SPLIT
    return [pl.BlockSpec((None, per_ref, tt), lambda i, j=j: (i, j, 0), memory_space=pltpu.SMEM)
            for j in range(RO_SPLIT)]


def _diag_mask(n_picks):
    return (lax.broadcasted_iota(I32, (SUBLANES, n_picks * SUBLANES), 1) % SUBLANES
            == lax.broadcasted_iota(I32, (SUBLANES, n_picks * SUBLANES), 0))


def _peer_u_kernel(*refs, layer):
    ro_refs = refs[:RO_SPLIT]
    g_ref, h8_ref, tabs_hbm, sum8_ref, w_ref, a8_scr, tab_ref, buf, slab_scr, sem = refs[RO_SPLIT:]
    tt, n_picks = g_ref.shape
    eye = _diag_mask(n_picks)

    @pl.when(pl.program_id(0) == 0)
    def _():
        _fill_table(tabs_hbm, layer, tab_ref, buf, slab_scr, sem)

    def token_group(tg, carry):
        rows = []
        ro_wins = _token_windows(ro_refs, tg)
        for tl in range(TOKENS_PER_ITER):
            t = tg * TOKENS_PER_ITER + tl
            bm = _gather_slabs(ro_wins, tab_ref, tl, n_picks)
            x8 = h8_ref[pl.ds(pl.multiple_of(t * SUBLANES, SUBLANES), SUBLANES), :]
            r = _nt_dot(x8, bm)
            rows.append(jnp.sum(jnp.where(eye, r, 0.0), axis=0, keepdims=True))
        a8_scr[pl.ds(pl.multiple_of(tg * TOKENS_PER_ITER, TOKENS_PER_ITER), TOKENS_PER_ITER), :] = (
            jnp.concatenate(rows, axis=0))
        return carry

    lax.fori_loop(0, tt // TOKENS_PER_ITER, token_group, 0)
    a8 = a8_scr[...]
    hi = a8.astype(BF16)
    lo = (a8 - hi.astype(F32)).astype(BF16)
    a = jnp.dot(hi, sum8_ref[...], preferred_element_type=F32) + jnp.dot(lo, sum8_ref[...], preferred_element_type=F32)
    gelu = 0.5 * a * (1.0 + lax.erf(a * np.float32(2.0 ** -0.5)))
    w_ref[...] = g_ref[...] * gelu


def _peer_u(ro, g, h8, tabs, layer, sum8):
    nt, n_picks, tt = ro.shape
    ro_parts = [ro] * RO_SPLIT
    return pl.pallas_call(
        functools.partial(_peer_u_kernel, layer=layer),
        grid=(nt,),
        in_specs=_offset_specs(tt, n_picks) + [
            pl.BlockSpec((tt, n_picks), lambda i: (i, 0)),
            pl.BlockSpec((tt * SUBLANES, LANES), lambda i: (i, 0)),
            pl.BlockSpec(memory_space=pl.ANY),
            pl.BlockSpec(sum8.shape, lambda i: (0, 0)),
        ],
        out_specs=pl.BlockSpec((tt, n_picks), lambda i: (i, 0)),
        out_shape=jax.ShapeDtypeStruct((nt * tt, n_picks), F32),
        scratch_shapes=[pltpu.VMEM((tt, n_picks * SUBLANES), F32)] + _table_scratch(*tabs.shape[1:]),
        compiler_params=pltpu.CompilerParams(dimension_semantics=("arbitrary",),
                                             vmem_limit_bytes=VMEM_TABLE_LIMIT),
        name="peer_u",
    )(*ro_parts, g, h8, tabs, sum8)


def _peer_v_kernel(*refs, layer):
    ro_refs = refs[:RO_SPLIT]
    (w_ref, tabs_hbm, rep8_ref, x1_ref, mod_ref, gf_ref, y_ref,
     whi_scr, wlo_scr, o8_scr, tab_ref, buf, slab_scr, sem) = refs[RO_SPLIT:]
    tt, n_picks = w_ref.shape

    @pl.when(pl.program_id(0) == 0)
    def _():
        _fill_table(tabs_hbm, layer, tab_ref, buf, slab_scr, sem)

    w = w_ref[...]
    hi = w.astype(BF16)
    lo = (w - hi.astype(F32)).astype(BF16)
    whi_scr[...] = jnp.dot(hi, rep8_ref[...], preferred_element_type=F32)
    wlo_scr[...] = jnp.dot(lo, rep8_ref[...], preferred_element_type=F32)
    eye = _diag_mask(n_picks)

    def token_group(tg, carry):
        ro_wins = _token_windows(ro_refs, tg)
        for tl in range(TOKENS_PER_ITER):
            t = tg * TOKENS_PER_ITER + tl
            bm = _gather_slabs(ro_wins, tab_ref, tl, n_picks)
            w_hi = jnp.where(eye, whi_scr[pl.ds(t, 1), :], 0.0).astype(BF16)
            w_lo = jnp.where(eye, wlo_scr[pl.ds(t, 1), :], 0.0).astype(BF16)
            r = jnp.dot(jnp.concatenate([w_hi, w_lo], axis=0), bm, preferred_element_type=F32)
            o8_scr[pl.ds(pl.multiple_of(t * SUBLANES, SUBLANES), SUBLANES), :] = r[0:SUBLANES] + r[SUBLANES:]
        return carry

    lax.fori_loop(0, tt // TOKENS_PER_ITER, token_group, 0)
    m = mod_ref[0]
    y_ref[...] = _rmsnorm(x1_ref[...] + m[5:6] * _slabs_to_rows(o8_scr), gf_ref[...])


def _peer_v(ro, w, tabs, layer, rep8, x1, mod, final_g):
    nt, n_picks, tt = ro.shape
    bsz, s, d = x1.shape
    tiles_per_batch = s // tt
    ro_parts = [ro] * RO_SPLIT
    return pl.pallas_call(
        functools.partial(_peer_v_kernel, layer=layer),
        grid=(nt,),
        in_specs=_offset_specs(tt, n_picks) + [
            pl.BlockSpec((tt, n_picks), lambda i: (i, 0)),
            pl.BlockSpec(memory_space=pl.ANY),
            pl.BlockSpec(rep8.shape, lambda i: (0, 0)),
            pl.BlockSpec((tt, d), lambda i: (i, 0)),
            pl.BlockSpec((1, N_MOD, d), lambda i: (i // tiles_per_batch, 0, 0)),
            pl.BlockSpec((1, d), lambda i: (0, 0)),
        ],
        out_specs=pl.BlockSpec((tt, d), lambda i: (i, 0)),
        out_shape=jax.ShapeDtypeStruct((nt * tt, d), F32),
        scratch_shapes=[pltpu.VMEM((tt, n_picks * SUBLANES), F32), pltpu.VMEM((tt, n_picks * SUBLANES), F32),
                        pltpu.VMEM((tt * SUBLANES, LANES), F32)] + _table_scratch(*tabs.shape[1:]),
        compiler_params=pltpu.CompilerParams(dimension_semantics=("arbitrary",),
                                             vmem_limit_bytes=VMEM_TABLE_LIMIT),
        name="peer_v",
    )(*ro_parts, w, tabs, rep8, x1.reshape(bsz * s, d), mod, final_g.reshape(1, d))


def _layer(layer, x, ctx, mods, norm1_g, w_in, pool_w, pool_scale, rpb, w_out, norm2_g, wq, keys, u_tabs, v_tabs,
           final_g):
    bsz, s, d = x.shape
    d_in = w_in.shape[1]
    d_pool = pool_scale.shape[0]
    d_attn = (d_in - d_pool) // 3
    mod_x = mods[:bsz].reshape(bsz, N_MOD, d)
    mod_c = mods[bsz:bsz + 1].reshape(1, N_MOD, d)
    w_in_b = w_in.astype(BF16)
    scale = float((d_attn // NA_HEADS) ** -0.5)

    tm = min(DENSE_TILE, s)
    p, q, k, vv = _in_proj(
        x, mod_x, norm1_g, w_in_b,
        splits=((0, d_pool, 1.0), (d_pool, d_pool + d_attn, scale),
                (d_pool + d_attn, d_pool + 2 * d_attn, 1.0), (d_pool + 2 * d_attn, d_in, 1.0)),
        dtypes=(F32, BF16, BF16, BF16), tm=tm, mod_per_batch=True, name="in_proj")
    kc, vc = _in_proj(
        ctx, mod_c, norm1_g, w_in_b[:, d_pool + d_attn:],
        splits=((0, d_attn, 1.0), (d_attn, 2 * d_attn, 1.0)),
        dtypes=(BF16, BF16), tm=ctx.shape[1], mod_per_batch=False, name="ctx_proj")

    attn = _attention(q, k, vv, kc, vc, _attn_bias(rpb))
    x1, h2, h8 = _mix(p, attn, x, mod_x, pool_w.astype(BF16), pool_scale, w_out.astype(BF16), norm2_g, tm)

    n = bsz * s
    ro, g = _retrieve(h2.reshape(n, d), wq.T.astype(BF16),
                      keys.reshape(-1, PEER_NKEYS, keys.shape[-1]).astype(BF16), tq=RETRIEVE_TILE)
    assert ro.shape[2] == PEER_TILE
    n_picks = PEER_HEADS * PEER_TOPK
    rep8 = _expand_matrix(n_picks)
    w = _peer_u(ro, g, h8, u_tabs, layer, rep8.T)
    y = _peer_v(ro, w, v_tabs, layer, rep8, x1, mod_x, final_g)
    return y.reshape(bsz, s, d)


def kernel(x, c, ctx, c_ctx, ada_w, ada_b, norm1_g, w_in, pool_w, pool_scale, na_rpb, w_out, norm2_g, peer_wq,
           peer_keys, peer_u, peer_v, final_g):
    depth = ada_w.shape[0]
    assert depth == 1, "context-stream update of deeper stacks is not implemented"
    bsz, s, d = x.shape
    rows_c = -(-(bsz + 1) // SUBLANES) * SUBLANES
    cc = jnp.concatenate([c, c_ctx[None, :], jnp.zeros((rows_c - bsz - 1, d), F32)], axis=0)
    mods = _ada(cc, ada_w[0], ada_b[0])
    return _layer(0, x, ctx, mods, norm1_g[0], w_in[0], pool_w[0], pool_scale[0], na_rpb[0], w_out[0],
                  norm2_g[0], peer_wq[0], peer_keys[0], peer_u, peer_v, final_g)
```
